```python
import jax, jax.numpy as jnp
from jax import lax
import numpy as np

D_MODEL = 2048
BATCH = 4
SEQ = 4096
DEPTH = 2

GRID_W = 64
CTX_LEN = 256
EPS = 1e-6

HEAD_DIM = 128
RET_HEADS = (D_MODEL // 2) // HEAD_DIM
RET_WIDTH = RET_HEADS * HEAD_DIM
RET_CHUNK = 128
QK_SCALE = HEAD_DIM ** -0.5
ROPE_BASE = 10000.0

SGU_CHUNK = 128
SGU_GROUPS = 8
SGU_GROUP_DIM = (D_MODEL // 2) // SGU_GROUPS
SGU_WIDTH = SGU_GROUPS * SGU_GROUP_DIM

AB_IN_WIDTH = 4 * RET_WIDTH + 2 * SGU_WIDTH
AB_OUT_WIDTH = RET_WIDTH + SGU_WIDTH

CONV_WIDTH = D_MODEL
CONV_K = 3

N_GROUPS = 4
EXPERTS_PER_GROUP = 4
N_EXPERTS = N_GROUPS * EXPERTS_PER_GROUP
TOP_K_INNER = 2
D_EXPERT = D_MODEL // 2

N_EVEN = (DEPTH + 1) // 2
N_ODD = DEPTH // 2

kernel_name = "hybrid_retention_sgu_shortconv_hmoe_dit"


def rmsnorm(x, g):
    x32 = x.astype(jnp.float32)
    y = x32 * lax.rsqrt(jnp.mean(x32 * x32, axis=-1, keepdims=True) + EPS)
    return (y * g.astype(jnp.float32)).astype(x.dtype)


def modulate(x, g, shift, scale):
    return rmsnorm(x, g) * (1 + scale) + shift


def head_groupnorm(y):
    y32 = y.astype(jnp.float32)
    mu = jnp.mean(y32, axis=-1, keepdims=True)
    var = jnp.mean(jnp.square(y32 - mu), axis=-1, keepdims=True)
    return (y32 - mu) * lax.rsqrt(var + EPS)


def axial_rope_tables(n_tokens):
    t = jnp.arange(n_tokens)
    row = (t // GRID_W).astype(jnp.float32)
    col = (t % GRID_W).astype(jnp.float32)
    n_freq = HEAD_DIM // 4
    inv = ROPE_BASE ** (-jnp.arange(n_freq, dtype=jnp.float32) / n_freq)
    ang = jnp.concatenate([row[:, None] * inv, col[:, None] * inv], axis=-1)
    return jnp.cos(ang), jnp.sin(ang)


def apply_rope(x, cos, sin):
    x1, x2 = jnp.split(x, 2, axis=-1)
    cos = cos.astype(x.dtype)
    sin = sin.astype(x.dtype)
    return jnp.concatenate([x1 * cos - x2 * sin, x1 * sin + x2 * cos], axis=-1)


def retention_chunkwise(q, k, v, log_gamma, s0, include_diag):
    B, H, T, d = q.shape
    C = RET_CHUNK
    N = T // C
    qc = q.reshape(B, H, N, C, d)
    kc = k.reshape(B, H, N, C, d)
    vc = v.reshape(B, H, N, C, d)
    lg = log_gamma.astype(jnp.float32)
    i = jnp.arange(C, dtype=jnp.float32)
    diff = i[:, None] - i[None, :]
    mask = (diff >= 0) if include_diag else (diff > 0)
    decay = jnp.exp(jnp.where(mask[None], diff[None] * lg[:, None, None], -jnp.inf))
    scores = jnp.einsum('bhnid,bhnjd->bhnij', qc, kc) * decay[None, :, None].astype(q.dtype)
    y_intra = jnp.einsum('bhnij,bhnjd->bhnid', scores, vc)
    q_decay = jnp.exp((i + 1.0)[None, :] * lg[:, None])
    k_decay = jnp.exp((C - 1.0 - i)[None, :] * lg[:, None])
    chunk_decay = jnp.exp(C * lg)
    u = jnp.einsum('bhnjd,bhnje->nbhde', kc * k_decay[None, :, None, :, None].astype(k.dtype), vc)
    u = u.astype(jnp.float32)

    def step(s, u_n):
        return chunk_decay[None, :, None, None] * s + u_n, s

    s_final, s_prev = lax.scan(step, s0, u)
    y_cross = jnp.einsum('bhnid,nbhde->bhnie', qc, s_prev) * q_decay[None, :, None, :, None]
    y = (y_intra + y_cross).reshape(B, H, T, d)
    return y, s_final


def bidir_retention(q, k, v, lg_fwd, lg_bwd, s0_fwd, s0_bwd):
    flip = lambda t: jnp.flip(t, axis=2)
    y_f, s_f = retention_chunkwise(q, k, v, lg_fwd, s0_fwd, True)
    y_b, s_b = retention_chunkwise(flip(q), flip(k), flip(v), lg_bwd, s0_bwd, False)
    return y_f + flip(y_b), s_f, s_b


def spatial_gating(u, v, w_s, b_s):
    B, T, _ = u.shape
    n = T // SGU_CHUNK
    v = v.reshape(B, n, SGU_CHUNK, SGU_GROUPS, SGU_GROUP_DIM)
    v32 = v.astype(jnp.float32)
    mu = jnp.mean(v32, axis=-1, keepdims=True)
    var = jnp.mean(jnp.square(v32 - mu), axis=-1, keepdims=True)
    vn = ((v32 - mu) * lax.rsqrt(var + EPS)).astype(v.dtype)
    s = jnp.einsum('gpq,bnqgc->bnpgc', w_s, vn) + b_s.T[None, None, :, :, None]
    return u * s.reshape(B, T, SGU_WIDTH)


def mixer_ab(a_lat, a_ctx, w_in, w_out, decay_logit, w_s, b_s, cos, sin, with_ctx):
    lg_f = jax.nn.log_sigmoid(decay_logit[0].astype(jnp.float32))
    lg_b = jax.nn.log_sigmoid(decay_logit[1].astype(jnp.float32))
    split_at = [RET_WIDTH, 2 * RET_WIDTH, 3 * RET_WIDTH, 4 * RET_WIDTH, 4 * RET_WIDTH + SGU_WIDTH]

    def heads(t):
        B, T, _ = t.shape
        return t.reshape(B, T, RET_HEADS, HEAD_DIM).transpose(0, 2, 1, 3)

    def unheads(t):
        B, H, T, d = t.shape
        return t.transpose(0, 2, 1, 3).reshape(B, T, H * d)

    def finish(z_g, y_ret, z_u, z_v):
        r = jax.nn.silu(z_g) * unheads(head_groupnorm(y_ret)).astype(z_g.dtype)
        s = spatial_gating(jax.nn.gelu(z_u), jax.nn.gelu(z_v), w_s, b_s)
        return jnp.concatenate([r, s], axis=-1) @ w_out

    z_c = a_ctx @ w_in
    q_c, k_c, v_c, g_c, u_c, vs_c = jnp.split(z_c, split_at, axis=-1)
    zeros = jnp.zeros((a_ctx.shape[0], RET_HEADS, HEAD_DIM, HEAD_DIM), jnp.float32)
    y_c, s_f, s_b = bidir_retention(heads(q_c) * QK_SCALE, heads(k_c), heads(v_c), lg_f, lg_b, zeros, zeros)
    out_c = finish(g_c, y_c, u_c, vs_c) if with_ctx else None

    z_l = a_lat @ w_in
    q_l, k_l, v_l, g_l, u_l, vs_l = jnp.split(z_l, split_at, axis=-1)
    q_l = apply_rope(heads(q_l), cos, sin) * QK_SCALE
    k_l = apply_rope(heads(k_l), cos, sin)
    y_l, _, _ = bidir_retention(q_l, k_l, heads(v_l), lg_f, lg_b, s_f, s_b)
    out_l = finish(g_l, y_l, u_l, vs_l)
    return out_l, out_c


def conv3_centred(h, w, b, axis):
    n = h.shape[axis]
    half = CONV_K // 2
    pad = [(0, 0)] * h.ndim
    pad[axis] = (half, half)
    hp = jnp.pad(h, pad)
    y = b
    for j in range(CONV_K):
        y = y + lax.slice_in_dim(hp, j, j + n, axis=axis) * w[j]
    return y


def mixer_c(h, w_in, conv_w, conv_b, w_out, rows):
    z = h @ w_in
    gate_b, gate_c, hv = jnp.split(z, 3, axis=-1)
    xg = gate_c * hv
    if rows is None:
        y = conv3_centred(xg, conv_w, conv_b, axis=1)
    else:
        B, T, W = xg.shape
        y = conv3_centred(xg.reshape(B, rows, GRID_W, W), conv_w, conv_b, axis=2).reshape(B, T, W)
    return (gate_b * y) @ w_out


def hier_moe(h, w_r1, b_r1, w_r2, b_r2, w1, w3, w2):
    logit1 = (h @ w_r1 + b_r1).astype(jnp.float32)
    p1 = jax.nn.softmax(logit1, axis=-1)
    grp = jnp.argmax(logit1, axis=-1)
    p_grp = jnp.take_along_axis(p1, grp[:, None], axis=-1)
    logit2 = (jnp.einsum('nd,gde->nge', h, w_r2) + b_r2).astype(jnp.float32)
    logit2 = jnp.take_along_axis(logit2, grp[:, None, None], axis=1)[:, 0]
    top_v, top_i = lax.top_k(logit2, TOP_K_INNER)
    w_sel = jax.nn.softmax(top_v, axis=-1) * p_grp
    expert_idx = grp[:, None] * EXPERTS_PER_GROUP + top_i
    comb = jnp.sum(jax.nn.one_hot(expert_idx, N_EXPERTS, dtype=jnp.float32) * w_sel[..., None], axis=1)
    comb = comb.astype(h.dtype)
    y = jnp.zeros_like(h)
    for e in range(N_EXPERTS):
        a = jax.nn.silu(h @ w1[e]) * (h @ w3[e])
        y = y + comb[:, e:e + 1] * (a @ w2[e])
    return y


def setup_inputs(seed: int = 0) -> dict:
    key = jax.random.key(seed)
    ks = jax.random.split(key, 26)
    f32 = jnp.float32
    D = D_MODEL

    def nrm(k, shape, scale):
        return jax.random.normal(k, shape, f32) * scale

    decay_base = jnp.asarray(np.log(2.0 ** (5 + np.arange(RET_HEADS)) - 1.0), f32)
    return {
        "x": nrm(ks[0], (BATCH, SEQ, D), 1.0),
        "c": nrm(ks[1], (BATCH, D), 1.0),
        "ctx": nrm(ks[2], (BATCH, CTX_LEN, D), 1.0),
        "c_ctx": nrm(ks[3], (D,), 1.0),
        "w_mod": nrm(ks[4], (DEPTH, D, 6 * D), 0.5 * D ** -0.5),
        "b_mod": nrm(ks[5], (DEPTH, 6 * D), 0.02),
        "g_mix": 1.0 + nrm(ks[6], (DEPTH, D), 0.02),
        "g_ffn": 1.0 + nrm(ks[7], (DEPTH, D), 0.02),
        "ab_w_in": nrm(ks[8], (N_EVEN, D, AB_IN_WIDTH), D ** -0.5),
        "ab_w_out": nrm(ks[9], (N_EVEN, AB_OUT_WIDTH, D), AB_OUT_WIDTH ** -0.5),
        "ret_decay_logit": decay_base[None, None, :] + nrm(ks[10], (N_EVEN, 2, RET_HEADS), 0.1),
        "sgu_w_s": nrm(ks[11], (N_EVEN, SGU_GROUPS, SGU_CHUNK, SGU_CHUNK), 0.5 * SGU_CHUNK ** -0.5),
        "sgu_b_s": 1.0 + nrm(ks[12], (N_EVEN, SGU_GROUPS, SGU_CHUNK), 0.02),
        "cv_w_in": nrm(ks[13], (N_ODD, D, 3 * CONV_WIDTH), D ** -0.5),
        "cv_conv_w": nrm(ks[14], (N_ODD, CONV_K, CONV_WIDTH), CONV_K ** -0.5),
        "cv_conv_b": nrm(ks[15], (N_ODD, CONV_WIDTH), 0.02),
        "cv_w_out": nrm(ks[16], (N_ODD, CONV_WIDTH, D), CONV_WIDTH ** -0.5),
        "moe_w_r1": nrm(ks[17], (DEPTH, D, N_GROUPS), D ** -0.5),
        "moe_b_r1": nrm(ks[18], (DEPTH, N_GROUPS), 0.01),
        "moe_w_r2": nrm(ks[19], (DEPTH, N_GROUPS, D, EXPERTS_PER_GROUP), D ** -0.5),
        "moe_b_r2": nrm(ks[20], (DEPTH, N_GROUPS, EXPERTS_PER_GROUP), 0.01),
        "moe_w1": nrm(ks[21], (DEPTH, N_EXPERTS, D, D_EXPERT), D ** -0.5),
        "moe_w3": nrm(ks[22], (DEPTH, N_EXPERTS, D, D_EXPERT), D ** -0.5),
        "moe_w2": nrm(ks[23], (DEPTH, N_EXPERTS, D_EXPERT, D), D_EXPERT ** -0.5),
        "g_final": 1.0 + nrm(ks[24], (D,), 0.02),
    }


def reference(x, c, ctx, c_ctx, w_mod, b_mod, g_mix, g_ffn,
              ab_w_in, ab_w_out, ret_decay_logit, sgu_w_s, sgu_b_s,
              cv_w_in, cv_conv_w, cv_conv_b, cv_w_out,
              moe_w_r1, moe_b_r1, moe_w_r2, moe_b_r2, moe_w1, moe_w3, moe_w2,
              g_final):
    B, T, D = x.shape
    rows = T // GRID_W
    cos, sin = axial_rope_tables(T)
    for i in range(DEPTH):
        last = i == DEPTH - 1
        even = i % 2 == 0
        use_ctx = even or not last
        mod = jax.nn.silu(c) @ w_mod[i] + b_mod[i]
        sh1, sc1, gt1, sh2, sc2, gt2 = jnp.split(mod[:, None, :], 6, axis=-1)
        a_lat = modulate(x, g_mix[i], sh1, sc1)
        if use_ctx:
            cmod = jax.nn.silu(c_ctx) @ w_mod[i] + b_mod[i]
            csh1, csc1, cgt1, csh2, csc2, cgt2 = jnp.split(cmod, 6)
            a_ctx = modulate(ctx, g_mix[i], csh1, csc1)
        j = i // 2
        if even:
            m_lat, m_ctx = mixer_ab(a_lat, a_ctx, ab_w_in[j], ab_w_out[j], ret_decay_logit[j],
                                    sgu_w_s[j], sgu_b_s[j], cos, sin, not last)
        else:
            m_lat = mixer_c(a_lat, cv_w_in[j], cv_conv_w[j], cv_conv_b[j], cv_w_out[j], rows)
            m_ctx = None if last else mixer_c(a_ctx, cv_w_in[j], cv_conv_w[j], cv_conv_b[j], cv_w_out[j], None)
        x = x + gt1 * m_lat
        f_lat = modulate(x, g_ffn[i], sh2, sc2).reshape(-1, D)
        if last:
            y = hier_moe(f_lat, moe_w_r1[i], moe_b_r1[i], moe_w_r2[i], moe_b_r2[i],
                         moe_w1[i], moe_w3[i], moe_w2[i])
            x = x + gt2 * y.reshape(B, T, D)
        else:
            ctx = ctx + cgt1 * m_ctx
            f_ctx = modulate(ctx, g_ffn[i], csh2, csc2).reshape(-1, D)
            y = hier_moe(jnp.concatenate([f_lat, f_ctx], axis=0), moe_w_r1[i], moe_b_r1[i],
                         moe_w_r2[i], moe_b_r2[i], moe_w1[i], moe_w3[i], moe_w2[i])
            x = x + gt2 * y[:B * T].reshape(B, T, D)
            ctx = ctx + cgt2 * y[B * T:].reshape(ctx.shape)
    return rmsnorm(x, g_final)
```

```python
import functools

import jax
import jax.numpy as jnp
import numpy as np
from jax import lax
from jax.experimental import pallas as pl
from jax.experimental.pallas import tpu as pltpu

F32 = jnp.float32
BF16 = jnp.bfloat16

EPS = 1e-6
GRID_W = 64
HEAD_DIM = 128
CHUNK = 128
ROPE_BASE = 10000.0
N_GROUPS = 4
EXPERTS_PER_GROUP = 4
N_PAIRS = 6
N_CLASSES = N_GROUPS * N_PAIRS
PAIR_LO = (0, 0, 0, 1, 1, 2)
PAIR_HI = (1, 2, 3, 2, 3, 3)
LANES = 128
ROW_BLOCK = 256
MOE_TILE = 128
COL_CHUNK = 512
VMEM_LIMIT = 56 * 1024 * 1024
GELU_C = float(np.float32(np.sqrt(2.0 / np.pi)))


def _cparams(n_axes):
    return pltpu.CompilerParams(dimension_semantics=("arbitrary",) * n_axes,
                                vmem_limit_bytes=VMEM_LIMIT)


def _resident(shape):
    zeros = (0,) * len(shape)
    return pl.BlockSpec(shape, lambda *_: zeros, pipeline_mode=pl.Buffered(1))


def _sigmoid(x):
    return 1.0 / (1.0 + jnp.exp(-x))


def _gelu_tanh(x):
    return x * (0.5 * (1.0 + jnp.tanh(GELU_C * (x + 0.044715 * (x * x * x)))))


def _modulate(x, g, shift, scale):
    ms = jnp.mean(x * x, axis=-1, keepdims=True)
    return (x * lax.rsqrt(ms + EPS) * g) * (1.0 + scale) + shift


def _standardise(y):
    mu = jnp.mean(y, axis=-1, keepdims=True)
    d = y - mu
    var = jnp.mean(d * d, axis=-1, keepdims=True)
    return d * lax.rsqrt(var + EPS)


def _mod_kernel(c_ref, w_ref, b_ref, o_ref):
    c = c_ref[...]
    s = (c * _sigmoid(c)).astype(BF16)
    o_ref[0] = jnp.dot(s, w_ref[0].astype(BF16), preferred_element_type=F32) + b_ref[0]


def _mod_vectors(cvec, w_mod, b_mod):
    depth, d, n = w_mod.shape
    bn = 1536
    return pl.pallas_call(
        _mod_kernel,
        grid=(depth, n // bn),
        in_specs=[pl.BlockSpec((8, d), lambda l, j: (0, 0)),
                  pl.BlockSpec((1, d, bn), lambda l, j: (l, 0, j)),
                  pl.BlockSpec((1, 1, bn), lambda l, j: (l, 0, j))],
        out_specs=pl.BlockSpec((1, 8, bn), lambda l, j: (l, 0, j)),
        out_shape=jax.ShapeDtypeStruct((depth, 8, n), F32),
        compiler_params=_cparams(2),
        name="mod_vectors",
    )(cvec, w_mod, b_mod.reshape(depth, 1, n))


def _ab_in_kernel(x_ref, mod_ref, g_ref, w_ref, cos_ref, sin_ref, z_ref, *, ret_width, qk_scale):
    mod = mod_ref[0, 0]
    a = _modulate(x_ref[0], g_ref[...], mod[0:1], mod[1:2]).astype(BF16)
    cosf = cos_ref[...]
    sinf = sin_ref[...]
    n_out = w_ref.shape[1]
    for j in range(n_out // COL_CHUNK):
        c0 = j * COL_CHUNK
        z = jnp.dot(a, w_ref[:, c0:c0 + COL_CHUNK], preferred_element_type=F32)
        if c0 < 2 * ret_width:
            parts = []
            for hh in range(COL_CHUNK // HEAD_DIM):
                zh = z[:, hh * HEAD_DIM:(hh + 1) * HEAD_DIM]
                zh = zh * cosf + pltpu.roll(zh, HEAD_DIM // 2, axis=1) * sinf
                if c0 < ret_width:
                    zh = zh * qk_scale
                parts.append(zh)
            z = jnp.concatenate(parts, axis=1)
        z_ref[0, :, c0:c0 + COL_CHUNK] = z.astype(BF16)


def _ab_in_proj(x_all, modvec, g_mix, w_in, cosf, sinf, n_lat_blocks):
    b, rows, d = x_all.shape
    n_out = w_in.shape[1]
    ret_width = n_out // 6
    kern = functools.partial(_ab_in_kernel, ret_width=ret_width, qk_scale=HEAD_DIM ** -0.5)
    return pl.pallas_call(
        kern,
        grid=(b, rows // ROW_BLOCK),
        in_specs=[pl.BlockSpec((1, ROW_BLOCK, d), lambda bi, i: (bi, i, 0)),
                  pl.BlockSpec((1, 1, 6, d), lambda bi, i: (bi, i // n_lat_blocks, 0, 0)),
                  _resident((1, d)),
                  _resident((d, n_out)),
                  pl.BlockSpec((ROW_BLOCK, HEAD_DIM), lambda bi, i: (i, 0)),
                  pl.BlockSpec((ROW_BLOCK, HEAD_DIM), lambda bi, i: (i, 0))],
        out_specs=pl.BlockSpec((1, ROW_BLOCK, n_out), lambda bi, i: (bi, i, 0)),
        out_shape=jax.ShapeDtypeStruct((b, rows, n_out), BF16),
        compiler_params=_cparams(2),
        name="ab_in_proj",
    )(x_all, modvec, g_mix, w_in, cosf, sinf)


def _ret_state_kernel(k_ref, v_ref, kdec_ref, cdec_ref, r_ref, s_scr, *, n_chunks):
    s_scr[...] = jnp.zeros_like(s_scr)
    kdec = kdec_ref[0]
    cdec = cdec_ref[0]

    def body(i, carry):
        c = n_chunks - 1 - i
        r_ref[0, c, 0] = s_scr[...].astype(BF16)
        rows = pl.ds(pl.multiple_of(c * CHUNK, CHUNK), CHUNK)
        kd = (k_ref[0, rows, :].astype(F32) * kdec).astype(BF16)
        u = lax.dot_general(kd, v_ref[0, rows, :], (((0,), (0,)), ((), ())), preferred_element_type=F32)
        s_scr[...] = s_scr[...] * cdec + u
        return carry

    lax.fori_loop(0, n_chunks, body, 0)


def _ret_bwd_states(z_all, kdec_b, cdec_b, n_heads):
    b, rows, _ = z_all.shape
    n_chunks = rows // CHUNK
    kern = functools.partial(_ret_state_kernel, n_chunks=n_chunks)
    return pl.pallas_call(
        kern,
        grid=(b, n_heads),
        in_specs=[pl.BlockSpec((1, rows, HEAD_DIM), lambda bi, h: (bi, 0, n_heads + h)),
                  pl.BlockSpec((1, rows, HEAD_DIM), lambda bi, h: (bi, 0, 2 * n_heads + h)),
                  pl.BlockSpec((1, CHUNK, HEAD_DIM), lambda bi, h: (h, 0, 0)),
                  pl.BlockSpec((1, 1, HEAD_DIM), lambda bi, h: (h, 0, 0))],
        out_specs=pl.BlockSpec((1, n_chunks, 1, HEAD_DIM, HEAD_DIM), lambda bi, h: (bi, 0, h, 0, 0)),
        out_shape=jax.ShapeDtypeStruct((b, n_chunks, n_heads, HEAD_DIM, HEAD_DIM), BF16),
        scratch_shapes=[pltpu.VMEM((HEAD_DIM, HEAD_DIM), F32)],
        compiler_params=_cparams(2),
        name="ret_bwd_states",
    )(z_all, z_all, kdec_b, cdec_b)


def _residual_ffn_in(x, m, mod, gffn, wr_ref, br_ref, xo_ref, f_ref, lg_ref):
    xn = x + mod[2:3] * m
    xo_ref[0] = xn
    f = _modulate(xn, gffn, mod[3:4], mod[4:5])
    f_ref[0] = f
    lg_ref[0] = jnp.dot(f.astype(BF16), wr_ref[...], preferred_element_type=F32) + br_ref[...]


def _ab_mix_kernel(z_ref, r_ref, x_ref, mod_ref, wout_ref, ws_ref, bs_ref, dmat_ref, qdf_ref, qdb_ref,
                   kdf_ref, cdf_ref, gffn_ref, wr_ref, br_ref, xo_ref, f_ref, lg_ref, s_scr, rs_scr,
                   *, n_heads, n_sgu):
    @pl.when(pl.program_id(1) == 0)
    def _():
        s_scr[...] = jnp.zeros_like(s_scr)

    ret_w = n_heads * HEAD_DIM
    sgu_w = n_sgu * HEAD_DIM
    nt = (((1,), (1,)), ((), ()))
    tn = (((0,), (0,)), ((), ()))
    for c in range(ROW_BLOCK // CHUNK):
        r0 = c * CHUNK
        for h in range(n_heads):
            c0 = h * HEAD_DIM
            q = z_ref[0, r0:r0 + CHUNK, c0:c0 + HEAD_DIM]
            k = z_ref[0, r0:r0 + CHUNK, ret_w + c0:ret_w + c0 + HEAD_DIM]
            v = z_ref[0, r0:r0 + CHUNK, 2 * ret_w + c0:2 * ret_w + c0 + HEAD_DIM]
            g = z_ref[0, r0:r0 + CHUNK, 3 * ret_w + c0:3 * ret_w + c0 + HEAD_DIM].astype(F32)
            sc = lax.dot_general(q, k, nt, preferred_element_type=F32) * dmat_ref[h]
            y = jnp.dot(sc.astype(BF16), v, preferred_element_type=F32)
            y = y + jnp.dot(q, s_scr[h].astype(BF16), preferred_element_type=F32) * qdf_ref[h]
            y = y + jnp.dot(q, r_ref[0, c, h], preferred_element_type=F32) * qdb_ref[h]
            r = (g * _sigmoid(g)) * _standardise(y)
            rs_scr[r0:r0 + CHUNK, c0:c0 + HEAD_DIM] = r.astype(BF16)
            kd = (k.astype(F32) * kdf_ref[h]).astype(BF16)
            s_scr[h] = s_scr[h] * cdf_ref[h] + lax.dot_general(kd, v, tn, preferred_element_type=F32)
        for gi in range(n_sgu):
            c0 = 4 * ret_w + gi * HEAD_DIM
            u = _gelu_tanh(z_ref[0, r0:r0 + CHUNK, c0:c0 + HEAD_DIM].astype(F32))
            vv = _gelu_tanh(z_ref[0, r0:r0 + CHUNK, sgu_w + c0:sgu_w + c0 + HEAD_DIM].astype(F32))
            vn = _standardise(vv).astype(BF16)
            sg = jnp.dot(ws_ref[gi], vn, preferred_element_type=F32) + bs_ref[gi]
            rs_scr[r0:r0 + CHUNK, ret_w + gi * HEAD_DIM:ret_w + (gi + 1) * HEAD_DIM] = (u * sg).astype(BF16)
    m = jnp.dot(rs_scr[...], wout_ref[...], preferred_element_type=F32)
    _residual_ffn_in(x_ref[0], m, mod_ref[0, 0], gffn_ref[...], wr_ref, br_ref, xo_ref, f_ref, lg_ref)


def _ab_mix(z_all, r_states, x_all, modvec, w_out, w_s, b_s_rows, tabs, g_ffn, w_r, b_r, n_lat_blocks):
    b, rows, d = x_all.shape
    n_out = z_all.shape[2]
    n_heads = r_states.shape[2]
    n_sgu = w_s.shape[0]
    nblk = rows // ROW_BLOCK
    cpb = ROW_BLOCK // CHUNK
    blk = lambda i: (i + n_lat_blocks) % nblk
    kern = functools.partial(_ab_mix_kernel, n_heads=n_heads, n_sgu=n_sgu)
    tab_spec = _resident((n_heads, CHUNK, HEAD_DIM))
    row_out = lambda w, dt: (pl.BlockSpec((1, ROW_BLOCK, w), lambda bi, i: (bi, blk(i), 0)),
                             jax.ShapeDtypeStruct((b, rows, w), dt))
    outs = [row_out(d, F32), row_out(d, F32), row_out(LANES, F32)]
    return pl.pallas_call(
        kern,
        grid=(b, nblk),
        in_specs=[pl.BlockSpec((1, ROW_BLOCK, n_out), lambda bi, i: (bi, blk(i), 0)),
                  pl.BlockSpec((1, cpb, n_heads, HEAD_DIM, HEAD_DIM), lambda bi, i: (bi, blk(i), 0, 0, 0)),
                  pl.BlockSpec((1, ROW_BLOCK, d), lambda bi, i: (bi, blk(i), 0)),
                  pl.BlockSpec((1, 1, 6, d), lambda bi, i: (bi, blk(i) // n_lat_blocks, 0, 0)),
                  _resident(w_out.shape),
                  _resident(w_s.shape),
                  _resident(b_s_rows.shape),
                  tab_spec, tab_spec, tab_spec, tab_spec,
                  _resident((n_heads, 1, HEAD_DIM)),
                  _resident((1, d)),
                  _resident(w_r.shape),
                  _resident((1, LANES))],
        out_specs=[o[0] for o in outs],
        out_shape=[o[1] for o in outs],
        scratch_shapes=[pltpu.VMEM((n_heads, HEAD_DIM, HEAD_DIM), F32),
                        pltpu.VMEM((ROW_BLOCK, d), BF16)],
        compiler_params=_cparams(2),
        name="ab_mix",
    )(z_all, r_states, x_all, modvec, w_out, w_s, b_s_rows, tabs["dmat"], tabs["qdf"], tabs["qdb"],
      tabs["kdf"], tabs["cdf"], g_ffn, w_r, b_r)


def _cv_in_kernel(x_ref, mod_ref, g_ref, w_ref, cw_ref, cb_ref, p_ref):
    mod = mod_ref[0, 0]
    a = _modulate(x_ref[0], g_ref[...], mod[0:1], mod[1:2]).astype(BF16)
    bm = a.shape[0]
    width = p_ref.shape[2]
    col = lax.broadcasted_iota(jnp.int32, (bm, COL_CHUNK), 0) % GRID_W
    for j in range(width // COL_CHUNK):
        c0 = j * COL_CHUNK
        gate_b = jnp.dot(a, w_ref[:, c0:c0 + COL_CHUNK], preferred_element_type=F32)
        gate_c = jnp.dot(a, w_ref[:, width + c0:width + c0 + COL_CHUNK], preferred_element_type=F32)
        hv = jnp.dot(a, w_ref[:, 2 * width + c0:2 * width + c0 + COL_CHUNK], preferred_element_type=F32)
        xg = gate_c * hv
        left = jnp.where(col == 0, 0.0, pltpu.roll(xg, 1, axis=0))
        right = jnp.where(col == GRID_W - 1, 0.0, pltpu.roll(xg, bm - 1, axis=0))
        y = cb_ref[:, c0:c0 + COL_CHUNK] + left * cw_ref[0:1, c0:c0 + COL_CHUNK]
        y = y + xg * cw_ref[1:2, c0:c0 + COL_CHUNK]
        y = y + right * cw_ref[2:3, c0:c0 + COL_CHUNK]
        p_ref[0, :, c0:c0 + COL_CHUNK] = (gate_b * y).astype(BF16)


def _cv_in_proj(x1, modvec, g_mix, w_in, conv_w, conv_b, n_lat_blocks):
    b, _, d = x1.shape
    width = conv_w.shape[1]
    t = n_lat_blocks * ROW_BLOCK
    return pl.pallas_call(
        _cv_in_kernel,
        grid=(b, n_lat_blocks),
        in_specs=[pl.BlockSpec((1, ROW_BLOCK, d), lambda bi, i: (bi, i, 0)),
                  pl.BlockSpec((1, 1, 6, d), lambda bi, i: (bi, 0, 0, 0)),
                  _resident((1, d)),
                  _resident(w_in.shape),
                  _resident(conv_w.shape),
                  _resident((1, width))],
        out_specs=pl.BlockSpec((1, ROW_BLOCK, width), lambda bi, i: (bi, i, 0)),
        out_shape=jax.ShapeDtypeStruct((b, t, width), BF16),
        compiler_params=_cparams(2),
        name="cv_in_proj",
    )(x1, modvec, g_mix, w_in, conv_w, conv_b)


def _cv_out_kernel(p_ref, x_ref, mod_ref, wout_ref, gffn_ref, wr_ref, br_ref, xo_ref, f_ref, lg_ref):
    m = jnp.dot(p_ref[0], wout_ref[...], preferred_element_type=F32)
    _residual_ffn_in(x_ref[0], m, mod_ref[0, 0], gffn_ref[...], wr_ref, br_ref, xo_ref, f_ref, lg_ref)


def _cv_out_proj(p, x1, modvec, w_out, g_ffn, w_r, b_r):
    b, t, width = p.shape
    d = w_out.shape[1]
    row_out = lambda w: (pl.BlockSpec((1, ROW_BLOCK, w), lambda bi, i: (bi, i, 0)),
                         jax.ShapeDtypeStruct((b, t, w), F32))
    outs = [row_out(d), row_out(d), row_out(LANES)]
    return pl.pallas_call(
        _cv_out_kernel,
        grid=(b, t // ROW_BLOCK),
        in_specs=[pl.BlockSpec((1, ROW_BLOCK, width), lambda bi, i: (bi, i, 0)),
                  pl.BlockSpec((1, ROW_BLOCK, d), lambda bi, i: (bi, i, 0)),
                  pl.BlockSpec((1, 1, 6, d), lambda bi, i: (bi, 0, 0, 0)),
                  _resident(w_out.shape),
                  _resident((1, d)),
                  _resident(w_r.shape),
                  _resident((1, LANES))],
        out_specs=[o[0] for o in outs],
        out_shape=[o[1] for o in outs],
        compiler_params=_cparams(2),
        name="cv_out_proj",
    )(p, x1, modvec, w_out, g_ffn, w_r, b_r)


def _route_kernel(lg_ref, info_ref, cnt_ref, cnt_scr):
    @pl.when(pl.program_id(0) == 0)
    def _():
        cnt_scr[...] = jnp.zeros_like(cnt_scr)

    lg = lg_ref[...]
    bm = lg.shape[0]
    lane = lax.broadcasted_iota(jnp.int32, lg.shape, 1).astype(F32)
    neg = -jnp.inf

    def first_max(vals):
        mx = jnp.max(vals, axis=1, keepdims=True)
        idx = jnp.min(jnp.where(vals == mx, lane, float(LANES)), axis=1, keepdims=True)
        return mx, idx

    is_grp = lane < N_GROUPS
    m1, grp = first_max(jnp.where(is_grp, lg, neg))
    den = jnp.sum(jnp.where(is_grp, jnp.exp(lg - m1), 0.0), axis=1, keepdims=True)
    p_grp = 1.0 / den
    base = N_GROUPS + EXPERTS_PER_GROUP * grp
    in_grp = (lane >= base) & (lane < base + EXPERTS_PER_GROUP)
    l2 = jnp.where(in_grp, lg, neg)
    v1, i1 = first_max(l2)
    v2, i2 = first_max(jnp.where(lane == i1, neg, l2))
    e = jnp.exp(v2 - v1)
    w_top1 = p_grp * (1.0 / (1.0 + e))
    w_top2 = p_grp * (e / (1.0 + e))
    first_lo = i1 < i2
    lo = jnp.where(first_lo, i1, i2) - base
    hi = jnp.where(first_lo, i2, i1) - base
    w_lo = jnp.where(first_lo, w_top1, w_top2)
    w_hi = jnp.where(first_lo, w_top2, w_top1)
    pair = jnp.where(lo == 0.0, hi - 1.0, jnp.where(lo == 1.0, hi + 1.0, 5.0))
    cls = grp * N_PAIRS + pair

    onehot = (lane == cls).astype(F32)
    tri = (lax.broadcasted_iota(jnp.int32, (bm, bm), 0) > lax.broadcasted_iota(jnp.int32, (bm, bm), 1))
    before = jnp.dot(tri.astype(BF16), onehot.astype(BF16), preferred_element_type=F32)
    rank = jnp.sum(onehot * (before + cnt_scr[...]), axis=1, keepdims=True)
    cnt_scr[...] = cnt_scr[...] + jnp.sum(onehot, axis=0, keepdims=True)
    cnt_ref[...] = cnt_scr[...]
    info = jnp.where(lane == 0, cls,
                     jnp.where(lane == 1, rank,
                               jnp.where(lane == 2, w_lo, jnp.where(lane == 3, w_hi, 0.0))))
    info_ref[...] = info


def _route(logits):
    n = logits.shape[0]
    return pl.pallas_call(
        _route_kernel,
        grid=(n // ROW_BLOCK,),
        in_specs=[pl.BlockSpec((ROW_BLOCK, LANES), lambda i: (i, 0))],
        out_specs=[pl.BlockSpec((ROW_BLOCK, LANES), lambda i: (i, 0)),
                   pl.BlockSpec((1, LANES), lambda i: (0, 0))],
        out_shape=[jax.ShapeDtypeStruct((n, LANES), F32), jax.ShapeDtypeStruct((1, LANES), F32)],
        scratch_shapes=[pltpu.VMEM((1, LANES), F32)],
        compiler_params=_cparams(1),
        name="moe_route",
    )(logits)


def _dispatch_plan(info, counts, n_tok):
    tm = MOE_TILE
    n_tiles = n_tok // tm + N_CLASSES
    n_rows = n_tiles * tm
    cls = info[:, 0].astype(jnp.int32)
    rank = info[:, 1].astype(jnp.int32)
    cnt = counts[0, :N_CLASSES].astype(jnp.int32)
    padded = ((cnt + tm - 1) // tm) * tm
    seg_end = jnp.cumsum(padded)
    seg_start = seg_end - padded
    pos = seg_start[cls] + rank
    tok = jnp.arange(n_tok, dtype=jnp.int32)
    src = jnp.zeros((n_rows,), jnp.int32).at[pos].set(tok)
    w_rows = jnp.zeros((n_rows, 2), F32).at[pos].set(info[:, 2:4])
    n_used = seg_end[-1] // tm
    tile = jnp.arange(n_tiles, dtype=jnp.int32)
    tile_row = jnp.minimum(tile, n_used - 1) * tm
    tile_cls = jnp.sum((seg_end[None, :] <= tile_row[:, None]).astype(jnp.int32), axis=1)
    tile_cls = jnp.minimum(tile_cls, N_CLASSES - 1)
    grp = tile_cls // N_PAIRS
    pair = tile_cls % N_PAIRS
    e_lo = grp * EXPERTS_PER_GROUP + jnp.asarray(PAIR_LO, jnp.int32)[pair]
    e_hi = grp * EXPERTS_PER_GROUP + jnp.asarray(PAIR_HI, jnp.int32)[pair]
    n_valid = jnp.clip(seg_start[tile_cls] + cnt[tile_cls] - tile * tm, 0, tm)
    n_valid = jnp.where(tile < n_used, n_valid, 0).astype(jnp.int32)
    return dict(src=src.reshape(n_tiles, 1, tm), w_rows=w_rows, e_lo=e_lo, e_hi=e_hi,
                n_valid=n_valid, n_used=n_used.reshape(1).astype(jnp.int32), n_tiles=n_tiles)


def _row_copy(src_hbm, src_row, dst_buf, dst_row, sem):
    return pltpu.make_async_copy(src_hbm.at[pl.ds(src_row, 1), :], dst_buf.at[pl.ds(dst_row, 1), :], sem)


def _moe_up_kernel(elo_ref, ehi_ref, nu_ref, src_ref, nxt_ref, f_hbm, ws_ref, w1a, w3a, w1b, w3b,
                   h_ref, xbuf, sems):
    i = pl.program_id(0)
    n_used = nu_ref[0]
    tm = xbuf.shape[1]
    slot = i % 2

    def gather(idx_ref, s):
        def body(r, carry):
            _row_copy(f_hbm, idx_ref[0, 0, r], xbuf.at[s], r, sems.at[s]).start()
            return carry
        lax.fori_loop(0, tm, body, 0)

    @pl.when(i == 0)
    def _():
        gather(src_ref, 0)

    @pl.when(i + 1 < n_used)
    def _():
        gather(nxt_ref, 1 - slot)

    @pl.when(i < n_used)
    def _():
        def wait(r, carry):
            _row_copy(f_hbm, 0, xbuf.at[slot], r, sems.at[slot]).wait()
            return carry
        lax.fori_loop(0, tm, wait, 0)
        x = xbuf[slot].astype(BF16)
        ws = ws_ref[...]
        half = w1a.shape[2]
        for w1, w3, col in ((w1a, w3a, 0), (w1b, w3b, 1)):
            up = jnp.dot(x, w1[0], preferred_element_type=F32)
            gate = jnp.dot(x, w3[0], preferred_element_type=F32)
            act = (up * _sigmoid(up)) * gate
            h_ref[:, col * half:(col + 1) * half] = (act * ws[:, col:col + 1]).astype(BF16)

    @pl.when(i >= n_used)
    def _():
        h_ref[...] = jnp.zeros_like(h_ref)


def _moe_up(f_rows, plan, w1, w3):
    n_tiles = plan["n_tiles"]
    tm = MOE_TILE
    d = f_rows.shape[1]
    n_e, _, de = w1.shape
    last = n_tiles - 1
    wspec = lambda which: pl.BlockSpec((1, d, de), lambda i, lo, hi, nu: ((lo, hi)[which][i], 0, 0))
    grid_spec = pltpu.PrefetchScalarGridSpec(
        num_scalar_prefetch=3,
        grid=(n_tiles,),
        in_specs=[pl.BlockSpec((1, 1, tm), lambda i, *_: (i, 0, 0), memory_space=pltpu.SMEM),
                  pl.BlockSpec((1, 1, tm), lambda i, *_: (jnp.minimum(i + 1, last), 0, 0),
                               memory_space=pltpu.SMEM),
                  pl.BlockSpec(memory_space=pl.ANY),
                  pl.BlockSpec((tm, 2), lambda i, *_: (i, 0)),
                  wspec(0), wspec(0), wspec(1), wspec(1)],
        out_specs=pl.BlockSpec((tm, 2 * de), lambda i, *_: (i, 0)),
        scratch_shapes=[pltpu.VMEM((2, tm, d), F32), pltpu.SemaphoreType.DMA((2,))],
    )
    return pl.pallas_call(
        _moe_up_kernel,
        grid_spec=grid_spec,
        out_shape=jax.ShapeDtypeStruct((n_tiles * tm, 2 * de), BF16),
        compiler_params=_cparams(1),
        name="moe_up",
    )(plan["e_lo"], plan["e_hi"], plan["n_used"], plan["src"], plan["src"], f_rows, plan["w_rows"],
      w1, w3, w1, w3)


def _moe_down_kernel(elo_ref, ehi_ref, nv_ref, dst_ref, h_ref, w2a, w2b, y_hbm, obuf, sems):
    i = pl.program_id(0)
    n_steps = pl.num_programs(0)
    slot = i % 2
    half = w2a.shape[1]

    def drain(tile, s):
        def body(r, carry):
            _row_copy(obuf.at[s], 0, y_hbm, 0, sems.at[s]).wait()
            return carry
        lax.fori_loop(0, nv_ref[tile], body, 0)

    @pl.when(i >= 2)
    def _():
        drain(i - 2, slot)

    n_valid = nv_ref[i]

    @pl.when(n_valid > 0)
    def _():
        out = jnp.dot(h_ref[:, :half], w2a[0], preferred_element_type=F32)
        out = out + jnp.dot(h_ref[:, half:], w2b[0], preferred_element_type=F32)
        obuf[slot] = out

        def scatter(r, carry):
            _row_copy(obuf.at[slot], r, y_hbm, dst_ref[0, 0, r], sems.at[slot]).start()
            return carry
        lax.fori_loop(0, n_valid, scatter, 0)

    @pl.when(i == n_steps - 1)
    def _():
        @pl.when(i >= 1)
        def _():
            drain(i - 1, 1 - slot)
        drain(i, slot)


def _moe_down(h_rows, plan, w2, n_tok):
    n_tiles = plan["n_tiles"]
    tm = MOE_TILE
    n_e, de, d = w2.shape
    wspec = lambda which: pl.BlockSpec((1, de, d), lambda i, lo, hi, nv: ((lo, hi)[which][i], 0, 0))
    grid_spec = pltpu.PrefetchScalarGridSpec(
        num_scalar_prefetch=3,
        grid=(n_tiles,),
        in_specs=[pl.BlockSpec((1, 1, tm), lambda i, *_: (i, 0, 0), memory_space=pltpu.SMEM),
                  pl.BlockSpec((tm, 2 * de), lambda i, *_: (i, 0)),
                  wspec(0), wspec(1)],
        out_specs=pl.BlockSpec(memory_space=pl.ANY),
        scratch_shapes=[pltpu.VMEM((2, tm, d), F32), pltpu.SemaphoreType.DMA((2,))],
    )
    return pl.pallas_call(
        _moe_down_kernel,
        grid_spec=grid_spec,
        out_shape=jax.ShapeDtypeStruct((n_tok, d), F32),
        compiler_params=_cparams(1),
        name="moe_down",
    )(plan["e_lo"], plan["e_hi"], plan["n_valid"], plan["src"], h_rows, w2, w2)


def _hier_moe(f_rows, logits, w1, w3, w2):
    n_tok = f_rows.shape[0]
    info, counts = _route(logits)
    plan = _dispatch_plan(info, counts, n_tok)
    h_rows = _moe_up(f_rows, plan, w1, w3)
    return _moe_down(h_rows, plan, w2, n_tok)


def _resid_kernel(x_ref, y_ref, mod_ref, o_ref):
    o_ref[0] = x_ref[0] + mod_ref[0, 0][5:6] * y_ref[0]


def _final_kernel(x_ref, y_ref, mod_ref, g_ref, o_ref):
    x = x_ref[0] + mod_ref[0, 0][5:6] * y_ref[0]
    ms = jnp.mean(x * x, axis=-1, keepdims=True)
    o_ref[0] = (x * lax.rsqrt(ms + EPS)) * g_ref[...]


def _gated_residual(x, y, modvec, n_lat_blocks, g_final=None):
    b, rows, d = x.shape
    row_spec = pl.BlockSpec((1, ROW_BLOCK, d), lambda bi, i: (bi, i, 0))
    in_specs = [row_spec, row_spec,
                pl.BlockSpec((1, 1, 6, d), lambda bi, i: (bi, i // n_lat_blocks, 0, 0))]
    args = [x, y, modvec]
    kern = _resid_kernel
    if g_final is not None:
        in_specs.append(_resident((1, d)))
        args.append(g_final)
        kern = _final_kernel
    return pl.pallas_call(
        kern,
        grid=(b, rows // ROW_BLOCK),
        in_specs=in_specs,
        out_specs=row_spec,
        out_shape=jax.ShapeDtypeStruct((b, rows, d), F32),
        compiler_params=_cparams(2),
        name="gated_residual" if g_final is None else "final_norm",
    )(*args)


def _rope_tables(t, ctx_len):
    pos = jnp.arange(t)
    row = (pos // GRID_W).astype(F32)
    col = (pos % GRID_W).astype(F32)
    n_freq = HEAD_DIM // 4
    inv = ROPE_BASE ** (-jnp.arange(n_freq, dtype=F32) / n_freq)
    ang = jnp.concatenate([row[:, None] * inv, col[:, None] * inv], axis=-1)
    cos, sin = jnp.cos(ang), jnp.sin(ang)
    cosf = jnp.concatenate([cos, cos], axis=-1)
    sinf = jnp.concatenate([-sin, sin], axis=-1)
    cosf = jnp.concatenate([cosf, jnp.ones((ctx_len, HEAD_DIM), F32)], axis=0)
    sinf = jnp.concatenate([sinf, jnp.zeros((ctx_len, HEAD_DIM), F32)], axis=0)
    return cosf, sinf


def _retention_tables(decay_logit):
    lg_f = jax.nn.log_sigmoid(decay_logit[0].astype(F32))
    lg_b = jax.nn.log_sigmoid(decay_logit[1].astype(F32))
    n_heads = lg_f.shape[0]
    i = jnp.arange(CHUNK, dtype=F32)
    diff = i[:, None] - i[None, :]
    fwd = jnp.exp(jnp.where(diff >= 0, diff[None] * lg_f[:, None, None], -jnp.inf))
    bwd = jnp.exp(jnp.where(diff < 0, -diff[None] * lg_b[:, None, None], -jnp.inf))
    rows = lambda v: jnp.broadcast_to(v[:, :, None], (n_heads, CHUNK, HEAD_DIM))
    return dict(
        dmat=fwd + bwd,
        qdf=rows(jnp.exp((i + 1.0)[None, :] * lg_f[:, None])),
        qdb=rows(jnp.exp((CHUNK - i)[None, :] * lg_b[:, None])),
        kdf=rows(jnp.exp((CHUNK - 1.0 - i)[None, :] * lg_f[:, None])),
        kdb=rows(jnp.exp(i[None, :] * lg_b[:, None])),
        cdf=jnp.broadcast_to(jnp.exp(CHUNK * lg_f)[:, None, None], (n_heads, 1, HEAD_DIM)),
        cdb=jnp.broadcast_to(jnp.exp(CHUNK * lg_b)[:, None, None], (n_heads, 1, HEAD_DIM)),
    )


def _router_weights(w_r1, b_r1, w_r2, b_r2):
    d = w_r1.shape[0]
    n_exp = N_GROUPS * EXPERTS_PER_GROUP
    w = jnp.concatenate([w_r1, w_r2.transpose(1, 0, 2).reshape(d, n_exp)], axis=1)
    b = jnp.concatenate([b_r1, b_r2.reshape(n_exp)])
    pad = LANES - w.shape[1]
    return jnp.pad(w, ((0, 0), (0, pad))).astype(BF16), jnp.pad(b, (0, pad)).reshape(1, LANES)


def kernel(x, c, ctx, c_ctx, w_mod, b_mod, g_mix, g_ffn, ab_w_in, ab_w_out, ret_decay_logit, sgu_w_s, sgu_b_s,
           cv_w_in, cv_conv_w, cv_conv_b, cv_w_out, moe_w_r1, moe_b_r1, moe_w_r2, moe_b_r2, moe_w1, moe_w3,
           moe_w2, g_final):
    b, t, d = x.shape
    ctx_len = ctx.shape[1]
    n_lat_blocks = t // ROW_BLOCK
    n_heads = ret_decay_logit.shape[2]

    cvec = jnp.zeros((8, d), F32).at[:b].set(c).at[b].set(c_ctx)
    mod = _mod_vectors(cvec, w_mod, b_mod)
    mod_lat = mod[:, :b].reshape(2, b, 1, 6, d)
    mod_ctx = jnp.broadcast_to(mod[:, b].reshape(2, 1, 1, 6, d), (2, b, 1, 6, d))
    modvec = jnp.concatenate([mod_lat, mod_ctx], axis=2)

    x_all = jnp.concatenate([x, ctx], axis=1)
    cosf, sinf = _rope_tables(t, ctx_len)
    tabs = _retention_tables(ret_decay_logit[0])
    z_all = _ab_in_proj(x_all, modvec[0], g_mix[0:1], ab_w_in[0].astype(BF16), cosf, sinf, n_lat_blocks)
    r_states = _ret_bwd_states(z_all, tabs["kdb"], tabs["cdb"], n_heads)
    b_s_rows = jnp.broadcast_to(sgu_b_s[0][:, :, None], sgu_b_s[0].shape + (HEAD_DIM,))
    w_r, b_r = _router_weights(moe_w_r1[0], moe_b_r1[0], moe_w_r2[0], moe_b_r2[0])
    x_mid, f0, lg0 = _ab_mix(z_all, r_states, x_all, modvec[0], ab_w_out[0].astype(BF16),
                             sgu_w_s[0].astype(BF16), b_s_rows, tabs, g_ffn[0:1], w_r, b_r, n_lat_blocks)
    n0 = b * (t + ctx_len)
    y0 = _hier_moe(f0.reshape(n0, d), lg0.reshape(n0, LANES), moe_w1[0].astype(BF16),
                   moe_w3[0].astype(BF16), moe_w2[0].astype(BF16))
    x1 = _gated_residual(x_mid, y0.reshape(b, t + ctx_len, d), modvec[0], n_lat_blocks)

    p = _cv_in_proj(x1, modvec[1], g_mix[1:2], cv_w_in[0].astype(BF16), cv_conv_w[0],
                    cv_conv_b[0:1], n_lat_blocks)
    w_r, b_r = _router_weights(moe_w_r1[1], moe_b_r1[1], moe_w_r2[1], moe_b_r2[1])
    x2, f1, lg1 = _cv_out_proj(p, x1, modvec[1], cv_w_out[0].astype(BF16), g_ffn[1:2], w_r, b_r)
    y1 = _hier_moe(f1.reshape(b * t, d), lg1.reshape(b * t, LANES), moe_w1[1].astype(BF16),
                   moe_w3[1].astype(BF16), moe_w2[1].astype(BF16))
    return _gated_residual(x2, y1.reshape(b, t, d), modvec[1], n_lat_blocks, g_final.reshape(1, d))
```

```python
import functools

import jax
import jax.numpy as jnp
import numpy as np
from jax import lax
from jax.experimental import pallas as pl
from jax.experimental.pallas import tpu as pltpu

F32 = jnp.float32
BF16 = jnp.bfloat16

EPS = 1e-6
GRID_W = 64
HEAD_DIM = 128
CHUNK = 128
ROPE_BASE = 10000.0
N_GROUPS = 4
EXPERTS_PER_GROUP = 4
N_PAIRS = 6
N_CLASSES = N_GROUPS * N_PAIRS
PAIR_LO = (0, 0, 0, 1, 1, 2)
PAIR_HI = (1, 2, 3, 2, 3, 3)
LANES = 128
ROW_BLOCK = 256
MOE_TILE = 128
COL_CHUNK = 512
STATE_HEADS_PER_STEP = 4
VMEM_LIMIT = 56 * 1024 * 1024
GELU_C = float(np.float32(np.sqrt(2.0 / np.pi)))


def _cparams(n_axes):
    return pltpu.CompilerParams(dimension_semantics=("arbitrary",) * n_axes,
                                vmem_limit_bytes=VMEM_LIMIT)


def _resident(shape):
    zeros = (0,) * len(shape)
    return pl.BlockSpec(shape, lambda *_: zeros, pipeline_mode=pl.Buffered(1))


def _sigmoid(x):
    return 1.0 / (1.0 + jnp.exp(-x))


def _gelu_tanh(x):
    return x * (0.5 * (1.0 + jnp.tanh(GELU_C * (x + 0.044715 * (x * x * x)))))


def _modulate(x, g, shift, scale):
    ms = jnp.mean(x * x, axis=-1, keepdims=True)
    return (x * lax.rsqrt(ms + EPS) * g) * (1.0 + scale) + shift


def _standardise(y):
    mu = jnp.mean(y, axis=-1, keepdims=True)
    d = y - mu
    var = jnp.mean(d * d, axis=-1, keepdims=True)
    return d * lax.rsqrt(var + EPS)


def _mod_kernel(c_ref, w_ref, b_ref, o_ref):
    c = c_ref[...]
    s = (c * _sigmoid(c)).astype(BF16)
    o_ref[0] = jnp.dot(s, w_ref[0].astype(BF16), preferred_element_type=F32) + b_ref[0]


def _mod_vectors(cvec, w_mod, b_mod):
    depth, d, n = w_mod.shape
    bn = 1536
    return pl.pallas_call(
        _mod_kernel,
        grid=(depth, n // bn),
        in_specs=[pl.BlockSpec((8, d), lambda l, j: (0, 0)),
                  pl.BlockSpec((1, d, bn), lambda l, j: (l, 0, j)),
                  pl.BlockSpec((1, 1, bn), lambda l, j: (l, 0, j))],
        out_specs=pl.BlockSpec((1, 8, bn), lambda l, j: (l, 0, j)),
        out_shape=jax.ShapeDtypeStruct((depth, 8, n), F32),
        compiler_params=_cparams(2),
        name="mod_vectors",
    )(cvec, w_mod, b_mod.reshape(depth, 1, n))


def _ab_in_kernel(x_ref, mod_ref, g_ref, w_ref, cos_ref, sin_ref, z_ref, *, ret_width, qk_scale):
    mod = mod_ref[0, 0]
    a = _modulate(x_ref[0], g_ref[...], mod[0:1], mod[1:2]).astype(BF16)
    cosf = cos_ref[...]
    sinf = sin_ref[...]
    n_out = w_ref.shape[1]
    for j in range(n_out // COL_CHUNK):
        c0 = j * COL_CHUNK
        z = jnp.dot(a, w_ref[:, c0:c0 + COL_CHUNK], preferred_element_type=F32)
        if c0 < 2 * ret_width:
            parts = []
            for hh in range(COL_CHUNK // HEAD_DIM):
                zh = z[:, hh * HEAD_DIM:(hh + 1) * HEAD_DIM]
                zh = zh * cosf + pltpu.roll(zh, HEAD_DIM // 2, axis=1) * sinf
                if c0 < ret_width:
                    zh = zh * qk_scale
                parts.append(zh)
            z = jnp.concatenate(parts, axis=1)
        z_ref[0, :, c0:c0 + COL_CHUNK] = z.astype(BF16)


def _ab_in_proj(x_all, modvec, g_mix, w_in, cosf, sinf, n_lat_blocks):
    b, rows, d = x_all.shape
    n_out = w_in.shape[1]
    ret_width = n_out // 6
    kern = functools.partial(_ab_in_kernel, ret_width=ret_width, qk_scale=HEAD_DIM ** -0.5)
    return pl.pallas_call(
        kern,
        grid=(b, rows // ROW_BLOCK),
        in_specs=[pl.BlockSpec((1, ROW_BLOCK, d), lambda bi, i: (bi, i, 0)),
                  pl.BlockSpec((1, 1, 6, d), lambda bi, i: (bi, i // n_lat_blocks, 0, 0)),
                  _resident((1, d)),
                  _resident((d, n_out)),
                  pl.BlockSpec((ROW_BLOCK, HEAD_DIM), lambda bi, i: (i, 0)),
                  pl.BlockSpec((ROW_BLOCK, HEAD_DIM), lambda bi, i: (i, 0))],
        out_specs=pl.BlockSpec((1, ROW_BLOCK, n_out), lambda bi, i: (bi, i, 0)),
        out_shape=jax.ShapeDtypeStruct((b, rows, n_out), BF16),
        compiler_params=_cparams(2),
        name="ab_in_proj",
    )(x_all, modvec, g_mix, w_in, cosf, sinf)


def _ret_state_kernel(k_ref, v_ref, kdec_ref, cdec_ref, r_ref, s_scr, *, n_chunks):
    s_scr[...] = jnp.zeros_like(s_scr)

    def body(i, carry):
        c = n_chunks - 1 - i
        rows = pl.ds(pl.multiple_of(c * CHUNK, CHUNK), CHUNK)
        for hh in range(s_scr.shape[0]):
            cols = slice(hh * HEAD_DIM, (hh + 1) * HEAD_DIM)
            r_ref[0, c, hh] = s_scr[hh].astype(BF16)
            kd = (k_ref[0, rows, cols].astype(F32) * kdec_ref[hh]).astype(BF16)
            u = lax.dot_general(kd, v_ref[0, rows, cols], (((0,), (0,)), ((), ())),
                                preferred_element_type=F32)
            s_scr[hh] = s_scr[hh] * cdec_ref[hh] + u
        return carry

    lax.fori_loop(0, n_chunks, body, 0)


def _ret_bwd_states(z_all, kdec_b, cdec_b, n_heads):
    b, rows, _ = z_all.shape
    n_chunks = rows // CHUNK
    hps = STATE_HEADS_PER_STEP
    wblk = hps * HEAD_DIM
    kern = functools.partial(_ret_state_kernel, n_chunks=n_chunks)
    return pl.pallas_call(
        kern,
        grid=(b, n_heads // hps),
        in_specs=[pl.BlockSpec((1, rows, wblk), lambda bi, g: (bi, 0, n_heads // hps + g)),
                  pl.BlockSpec((1, rows, wblk), lambda bi, g: (bi, 0, 2 * (n_heads // hps) + g)),
                  pl.BlockSpec((hps, CHUNK, HEAD_DIM), lambda bi, g: (g, 0, 0)),
                  pl.BlockSpec((hps, 1, HEAD_DIM), lambda bi, g: (g, 0, 0))],
        out_specs=pl.BlockSpec((1, n_chunks, hps, HEAD_DIM, HEAD_DIM), lambda bi, g: (bi, 0, g, 0, 0)),
        out_shape=jax.ShapeDtypeStruct((b, n_chunks, n_heads, HEAD_DIM, HEAD_DIM), BF16),
        scratch_shapes=[pltpu.VMEM((hps, HEAD_DIM, HEAD_DIM), F32)],
        compiler_params=_cparams(2),
        name="ret_bwd_states",
    )(z_all, z_all, kdec_b, cdec_b)


def _residual_ffn_in(x, m, mod, gffn, wr_ref, br_ref, xo_ref, f_ref, lg_ref):
    xn = x + mod[2:3] * m
    xo_ref[0] = xn
    f = _modulate(xn, gffn, mod[3:4], mod[4:5])
    f_ref[0] = f
    lg_ref[0] = jnp.dot(f.astype(BF16), wr_ref[...], preferred_element_type=F32) + br_ref[...]


def _ab_mix_kernel(z_ref, r_ref, x_ref, mod_ref, wout_ref, ws_ref, bs_ref, dmat_ref, qdf_ref, qdb_ref,
                   kdf_ref, cdf_ref, gffn_ref, wr_ref, br_ref, xo_ref, f_ref, lg_ref, s_scr, rs_scr,
                   *, n_heads, n_sgu):
    @pl.when(pl.program_id(1) == 0)
    def _():
        s_scr[...] = jnp.zeros_like(s_scr)

    ret_w = n_heads * HEAD_DIM
    sgu_w = n_sgu * HEAD_DIM
    nt = (((1,), (1,)), ((), ()))
    tn = (((0,), (0,)), ((), ()))
    for c in range(ROW_BLOCK // CHUNK):
        r0 = c * CHUNK
        for h in range(n_heads):
            c0 = h * HEAD_DIM
            q = z_ref[0, r0:r0 + CHUNK, c0:c0 + HEAD_DIM]
            k = z_ref[0, r0:r0 + CHUNK, ret_w + c0:ret_w + c0 + HEAD_DIM]
            v = z_ref[0, r0:r0 + CHUNK, 2 * ret_w + c0:2 * ret_w + c0 + HEAD_DIM]
            g = z_ref[0, r0:r0 + CHUNK, 3 * ret_w + c0:3 * ret_w + c0 + HEAD_DIM].astype(F32)
            sc = lax.dot_general(q, k, nt, preferred_element_type=F32) * dmat_ref[h]
            y = jnp.dot(sc.astype(BF16), v, preferred_element_type=F32)
            y = y + jnp.dot(q, s_scr[h].astype(BF16), preferred_element_type=F32) * qdf_ref[h]
            y = y + jnp.dot(q, r_ref[0, c, h], preferred_element_type=F32) * qdb_ref[h]
            r = (g * _sigmoid(g)) * _standardise(y)
            rs_scr[r0:r0 + CHUNK, c0:c0 + HEAD_DIM] = r.astype(BF16)
            kd = (k.astype(F32) * kdf_ref[h]).astype(BF16)
            s_scr[h] = s_scr[h] * cdf_ref[h] + lax.dot_general(kd, v, tn, preferred_element_type=F32)
        for gi in range(n_sgu):
            c0 = 4 * ret_w + gi * HEAD_DIM
            u = _gelu_tanh(z_ref[0, r0:r0 + CHUNK, c0:c0 + HEAD_DIM].astype(F32))
            vv = _gelu_tanh(z_ref[0, r0:r0 + CHUNK, sgu_w + c0:sgu_w + c0 + HEAD_DIM].astype(F32))
            vn = _standardise(vv).astype(BF16)
            sg = jnp.dot(ws_ref[gi], vn, preferred_element_type=F32) + bs_ref[gi]
            rs_scr[r0:r0 + CHUNK, ret_w + gi * HEAD_DIM:ret_w + (gi + 1) * HEAD_DIM] = (u * sg).astype(BF16)
    m = jnp.dot(rs_scr[...], wout_ref[...], preferred_element_type=F32)
    _residual_ffn_in(x_ref[0], m, mod_ref[0, 0], gffn_ref[...], wr_ref, br_ref, xo_ref, f_ref, lg_ref)


def _ab_mix(z_all, r_states, x_all, modvec, w_out, w_s, b_s_rows, tabs, g_ffn, w_r, b_r, n_lat_blocks):
    b, rows, d = x_all.shape
    n_out = z_all.shape[2]
    n_heads = r_states.shape[2]
    n_sgu = w_s.shape[0]
    nblk = rows // ROW_BLOCK
    cpb = ROW_BLOCK // CHUNK
    blk = lambda i: (i + n_lat_blocks) % nblk
    kern = functools.partial(_ab_mix_kernel, n_heads=n_heads, n_sgu=n_sgu)
    tab_spec = _resident((n_heads, CHUNK, HEAD_DIM))
    row_out = lambda w, dt: (pl.BlockSpec((1, ROW_BLOCK, w), lambda bi, i: (bi, blk(i), 0)),
                             jax.ShapeDtypeStruct((b, rows, w), dt))
    outs = [row_out(d, F32), row_out(d, F32), row_out(LANES, F32)]
    return pl.pallas_call(
        kern,
        grid=(b, nblk),
        in_specs=[pl.BlockSpec((1, ROW_BLOCK, n_out), lambda bi, i: (bi, blk(i), 0)),
                  pl.BlockSpec((1, cpb, n_heads, HEAD_DIM, HEAD_DIM), lambda bi, i: (bi, blk(i), 0, 0, 0)),
                  pl.BlockSpec((1, ROW_BLOCK, d), lambda bi, i: (bi, blk(i), 0)),
                  pl.BlockSpec((1, 1, 6, d), lambda bi, i: (bi, blk(i) // n_lat_blocks, 0, 0)),
                  _resident(w_out.shape),
                  _resident(w_s.shape),
                  _resident(b_s_rows.shape),
                  tab_spec, tab_spec, tab_spec, tab_spec,
                  _resident((n_heads, 1, HEAD_DIM)),
                  _resident((1, d)),
                  _resident(w_r.shape),
                  _resident((1, LANES))],
        out_specs=[o[0] for o in outs],
        out_shape=[o[1] for o in outs],
        scratch_shapes=[pltpu.VMEM((n_heads, HEAD_DIM, HEAD_DIM), F32),
                        pltpu.VMEM((ROW_BLOCK, d), BF16)],
        compiler_params=_cparams(2),
        name="ab_mix",
    )(z_all, r_states, x_all, modvec, w_out, w_s, b_s_rows, tabs["dmat"], tabs["qdf"], tabs["qdb"],
      tabs["kdf"], tabs["cdf"], g_ffn, w_r, b_r)


def _cv_in_kernel(x_ref, y_ref, mod0_ref, mod_ref, g_ref, w_ref, cw_ref, cb_ref, p_ref):
    mod = mod_ref[0, 0]
    x1 = x_ref[0] + mod0_ref[0, 0][5:6] * y_ref[...]
    a = _modulate(x1, g_ref[...], mod[0:1], mod[1:2]).astype(BF16)
    bm = a.shape[0]
    width = p_ref.shape[2]
    col = lax.broadcasted_iota(jnp.int32, (bm, COL_CHUNK), 0) % GRID_W
    for j in range(width // COL_CHUNK):
        c0 = j * COL_CHUNK
        gate_b = jnp.dot(a, w_ref[:, c0:c0 + COL_CHUNK], preferred_element_type=F32)
        gate_c = jnp.dot(a, w_ref[:, width + c0:width + c0 + COL_CHUNK], preferred_element_type=F32)
        hv = jnp.dot(a, w_ref[:, 2 * width + c0:2 * width + c0 + COL_CHUNK], preferred_element_type=F32)
        xg = gate_c * hv
        left = jnp.where(col == 0, 0.0, pltpu.roll(xg, 1, axis=0))
        right = jnp.where(col == GRID_W - 1, 0.0, pltpu.roll(xg, bm - 1, axis=0))
        y = cb_ref[:, c0:c0 + COL_CHUNK] + left * cw_ref[0:1, c0:c0 + COL_CHUNK]
        y = y + xg * cw_ref[1:2, c0:c0 + COL_CHUNK]
        y = y + right * cw_ref[2:3, c0:c0 + COL_CHUNK]
        p_ref[0, :, c0:c0 + COL_CHUNK] = (gate_b * y).astype(BF16)


def _cv_in_proj(x_mid, y_rows, modvec_prev, modvec, g_mix, w_in, conv_w, conv_b, n_lat_blocks):
    b, rows, d = x_mid.shape
    nblk = rows // ROW_BLOCK
    width = conv_w.shape[1]
    t = n_lat_blocks * ROW_BLOCK
    lat_mod = pl.BlockSpec((1, 1, 6, d), lambda bi, i: (bi, 0, 0, 0))
    return pl.pallas_call(
        _cv_in_kernel,
        grid=(b, n_lat_blocks),
        in_specs=[pl.BlockSpec((1, ROW_BLOCK, d), lambda bi, i: (bi, i, 0)),
                  pl.BlockSpec((ROW_BLOCK, d), lambda bi, i: (bi * nblk + i, 0)),
                  lat_mod, lat_mod,
                  _resident((1, d)),
                  _resident(w_in.shape),
                  _resident(conv_w.shape),
                  _resident((1, width))],
        out_specs=pl.BlockSpec((1, ROW_BLOCK, width), lambda bi, i: (bi, i, 0)),
        out_shape=jax.ShapeDtypeStruct((b, t, width), BF16),
        compiler_params=_cparams(2),
        name="cv_in_proj",
    )(x_mid, y_rows, modvec_prev, modvec, g_mix, w_in, conv_w, conv_b)


def _cv_out_kernel(p_ref, x_ref, y_ref, mod0_ref, mod_ref, wout_ref, gffn_ref, wr_ref, br_ref,
                   xo_ref, f_ref, lg_ref):
    x1 = x_ref[0] + mod0_ref[0, 0][5:6] * y_ref[...]
    m = jnp.dot(p_ref[0], wout_ref[...], preferred_element_type=F32)
    _residual_ffn_in(x1, m, mod_ref[0, 0], gffn_ref[...], wr_ref, br_ref, xo_ref, f_ref, lg_ref)


def _cv_out_proj(p, x_mid, y_rows, modvec_prev, modvec, w_out, g_ffn, w_r, b_r):
    b, t, width = p.shape
    d = w_out.shape[1]
    nblk = x_mid.shape[1] // ROW_BLOCK
    row_out = lambda w: (pl.BlockSpec((1, ROW_BLOCK, w), lambda bi, i: (bi, i, 0)),
                         jax.ShapeDtypeStruct((b, t, w), F32))
    outs = [row_out(d), row_out(d), row_out(LANES)]
    lat_mod = pl.BlockSpec((1, 1, 6, d), lambda bi, i: (bi, 0, 0, 0))
    return pl.pallas_call(
        _cv_out_kernel,
        grid=(b, t // ROW_BLOCK),
        in_specs=[pl.BlockSpec((1, ROW_BLOCK, width), lambda bi, i: (bi, i, 0)),
                  pl.BlockSpec((1, ROW_BLOCK, d), lambda bi, i: (bi, i, 0)),
                  pl.BlockSpec((ROW_BLOCK, d), lambda bi, i: (bi * nblk + i, 0)),
                  lat_mod, lat_mod,
                  _resident(w_out.shape),
                  _resident((1, d)),
                  _resident(w_r.shape),
                  _resident((1, LANES))],
        out_specs=[o[0] for o in outs],
        out_shape=[o[1] for o in outs],
        compiler_params=_cparams(2),
        name="cv_out_proj",
    )(p, x_mid, y_rows, modvec_prev, modvec, w_out, g_ffn, w_r, b_r)


def _route_kernel(lg_ref, info_ref, cnt_ref, cnt_scr):
    @pl.when(pl.program_id(0) == 0)
    def _():
        cnt_scr[...] = jnp.zeros_like(cnt_scr)

    lg = lg_ref[...]
    bm = lg.shape[0]
    lane = lax.broadcasted_iota(jnp.int32, lg.shape, 1).astype(F32)
    neg = -jnp.inf

    def first_max(vals):
        mx = jnp.max(vals, axis=1, keepdims=True)
        idx = jnp.min(jnp.where(vals == mx, lane, float(LANES)), axis=1, keepdims=True)
        return mx, idx

    is_grp = lane < N_GROUPS
    m1, grp = first_max(jnp.where(is_grp, lg, neg))
    den = jnp.sum(jnp.where(is_grp, jnp.exp(lg - m1), 0.0), axis=1, keepdims=True)
    p_grp = 1.0 / den
    base = N_GROUPS + EXPERTS_PER_GROUP * grp
    in_grp = (lane >= base) & (lane < base + EXPERTS_PER_GROUP)
    l2 = jnp.where(in_grp, lg, neg)
    v1, i1 = first_max(l2)
    v2, i2 = first_max(jnp.where(lane == i1, neg, l2))
    e = jnp.exp(v2 - v1)
    w_top1 = p_grp * (1.0 / (1.0 + e))
    w_top2 = p_grp * (e / (1.0 + e))
    first_lo = i1 < i2
    lo = jnp.where(first_lo, i1, i2) - base
    hi = jnp.where(first_lo, i2, i1) - base
    w_lo = jnp.where(first_lo, w_top1, w_top2)
    w_hi = jnp.where(first_lo, w_top2, w_top1)
    pair = jnp.where(lo == 0.0, hi - 1.0, jnp.where(lo == 1.0, hi + 1.0, 5.0))
    cls = grp * N_PAIRS + pair

    onehot = (lane == cls).astype(F32)
    tri = (lax.broadcasted_iota(jnp.int32, (bm, bm), 0) > lax.broadcasted_iota(jnp.int32, (bm, bm), 1))
    before = jnp.dot(tri.astype(BF16), onehot.astype(BF16), preferred_element_type=F32)
    rank = jnp.sum(onehot * (before + cnt_scr[...]), axis=1, keepdims=True)
    cnt_scr[...] = cnt_scr[...] + jnp.sum(onehot, axis=0, keepdims=True)
    cnt_ref[...] = cnt_scr[...]
    info = jnp.where(lane == 0, cls,
                     jnp.where(lane == 1, rank,
                               jnp.where(lane == 2, w_lo, jnp.where(lane == 3, w_hi, 0.0))))
    info_ref[...] = info


def _route(logits):
    n = logits.shape[0]
    return pl.pallas_call(
        _route_kernel,
        grid=(n // ROW_BLOCK,),
        in_specs=[pl.BlockSpec((ROW_BLOCK, LANES), lambda i: (i, 0))],
        out_specs=[pl.BlockSpec((ROW_BLOCK, LANES), lambda i: (i, 0)),
                   pl.BlockSpec((1, LANES), lambda i: (0, 0))],
        out_shape=[jax.ShapeDtypeStruct((n, LANES), F32), jax.ShapeDtypeStruct((1, LANES), F32)],
        scratch_shapes=[pltpu.VMEM((1, LANES), F32)],
        compiler_params=_cparams(1),
        name="moe_route",
    )(logits)


def _dispatch_plan(info, counts, n_tok):
    tm = MOE_TILE
    n_tiles = n_tok // tm + N_CLASSES
    n_rows = n_tiles * tm
    cls = info[:, 0].astype(jnp.int32)
    rank = info[:, 1].astype(jnp.int32)
    cnt = counts[0, :N_CLASSES].astype(jnp.int32)
    padded = ((cnt + tm - 1) // tm) * tm
    seg_end = jnp.cumsum(padded)
    seg_start = seg_end - padded
    pos = seg_start[cls] + rank
    tok = jnp.arange(n_tok, dtype=F32)[:, None]
    per_row = jnp.zeros((n_rows, 3), F32).at[pos].set(jnp.concatenate([tok, info[:, 2:4]], axis=1))
    src = per_row[:, 0].astype(jnp.int32)
    w_rows = per_row[:, 1:3]
    n_used = seg_end[-1] // tm
    tile = jnp.arange(n_tiles, dtype=jnp.int32)
    tile_row = jnp.minimum(tile, n_used - 1) * tm
    tile_cls = jnp.sum((seg_end[None, :] <= tile_row[:, None]).astype(jnp.int32), axis=1)
    tile_cls = jnp.minimum(tile_cls, N_CLASSES - 1)
    grp = tile_cls // N_PAIRS
    pair = tile_cls % N_PAIRS
    e_lo = grp * EXPERTS_PER_GROUP + jnp.asarray(PAIR_LO, jnp.int32)[pair]
    e_hi = grp * EXPERTS_PER_GROUP + jnp.asarray(PAIR_HI, jnp.int32)[pair]
    n_valid = jnp.clip(seg_start[tile_cls] + cnt[tile_cls] - tile * tm, 0, tm)
    n_valid = jnp.where(tile < n_used, n_valid, 0).astype(jnp.int32)
    src = src.reshape(n_tiles, tm)
    row = jnp.arange(tm, dtype=jnp.int32)[None, :]
    dump = n_tok + (tile % 2)[:, None] * tm + row
    dst = jnp.where(row < n_valid[:, None], src, dump)
    return dict(src=src.reshape(n_tiles, 1, tm), dst=dst.reshape(n_tiles, 1, tm), w_rows=w_rows,
                e_lo=e_lo, e_hi=e_hi, n_valid=n_valid, n_used=n_used.reshape(1).astype(jnp.int32),
                n_tiles=n_tiles)


def _row_copy(src_hbm, src_row, dst_buf, dst_row, sem):
    return pltpu.make_async_copy(src_hbm.at[pl.ds(src_row, 1), :], dst_buf.at[pl.ds(dst_row, 1), :], sem)


def _moe_up_kernel(elo_ref, ehi_ref, nu_ref, src_ref, nxt_ref, f_hbm, ws_ref, w1a, w3a, w1b, w3b,
                   h_ref, xbuf, sems):
    i = pl.program_id(0)
    n_used = nu_ref[0]
    tm = xbuf.shape[1]
    slot = i % 2

    def gather(idx_ref, s):
        for r in range(tm):
            _row_copy(f_hbm, idx_ref[0, 0, r], xbuf.at[s], r, sems.at[s]).start()

    @pl.when(i == 0)
    def _():
        gather(src_ref, 0)

    @pl.when(i + 1 < n_used)
    def _():
        gather(nxt_ref, 1 - slot)

    @pl.when(i < n_used)
    def _():
        pltpu.make_async_copy(f_hbm.at[pl.ds(0, tm), :], xbuf.at[slot], sems.at[slot]).wait()
        x = xbuf[slot].astype(BF16)
        ws = ws_ref[...]
        half = w1a.shape[2]
        for w1, w3, col in ((w1a, w3a, 0), (w1b, w3b, 1)):
            up = jnp.dot(x, w1[0], preferred_element_type=F32)
            gate = jnp.dot(x, w3[0], preferred_element_type=F32)
            act = (up * _sigmoid(up)) * gate
            h_ref[:, col * half:(col + 1) * half] = (act * ws[:, col:col + 1]).astype(BF16)

    @pl.when(i >= n_used)
    def _():
        h_ref[...] = jnp.zeros_like(h_ref)


def _moe_up(f_rows, plan, w1, w3):
    n_tiles = plan["n_tiles"]
    tm = MOE_TILE
    d = f_rows.shape[1]
    n_e, _, de = w1.shape
    last = n_tiles - 1
    wspec = lambda which: pl.BlockSpec((1, d, de), lambda i, lo, hi, nu: ((lo, hi)[which][i], 0, 0))
    grid_spec = pltpu.PrefetchScalarGridSpec(
        num_scalar_prefetch=3,
        grid=(n_tiles,),
        in_specs=[pl.BlockSpec((1, 1, tm), lambda i, *_: (i, 0, 0), memory_space=pltpu.SMEM),
                  pl.BlockSpec((1, 1, tm), lambda i, *_: (jnp.minimum(i + 1, last), 0, 0),
                               memory_space=pltpu.SMEM),
                  pl.BlockSpec(memory_space=pl.ANY),
                  pl.BlockSpec((tm, 2), lambda i, *_: (i, 0)),
                  wspec(0), wspec(0), wspec(1), wspec(1)],
        out_specs=pl.BlockSpec((tm, 2 * de), lambda i, *_: (i, 0)),
        scratch_shapes=[pltpu.VMEM((2, tm, d), F32), pltpu.SemaphoreType.DMA((2,))],
    )
    return pl.pallas_call(
        _moe_up_kernel,
        grid_spec=grid_spec,
        out_shape=jax.ShapeDtypeStruct((n_tiles * tm, 2 * de), BF16),
        compiler_params=_cparams(1),
        name="moe_up",
    )(plan["e_lo"], plan["e_hi"], plan["n_used"], plan["src"], plan["src"], f_rows, plan["w_rows"],
      w1, w3, w1, w3)


def _moe_down_kernel(elo_ref, ehi_ref, nv_ref, dst_ref, h_ref, w2a, w2b, y_hbm, obuf, sems):
    i = pl.program_id(0)
    n_steps = pl.num_programs(0)
    slot = i % 2
    half = w2a.shape[1]
    tm = obuf.shape[1]

    def drain(tile, s):
        @pl.when(nv_ref[tile] > 0)
        def _():
            pltpu.make_async_copy(obuf.at[s], y_hbm.at[pl.ds(0, tm), :], sems.at[s]).wait()

    @pl.when(i == 0)
    def _():
        n_tok = y_hbm.shape[0] - 2 * tm
        obuf[...] = jnp.zeros_like(obuf)
        for s in range(2):
            fill = pltpu.make_async_copy(obuf.at[s], y_hbm.at[pl.ds(n_tok + s * tm, tm), :], sems.at[s])
            fill.start()
            fill.wait()

    @pl.when(i >= 2)
    def _():
        drain(i - 2, slot)

    @pl.when(nv_ref[i] > 0)
    def _():
        out = jnp.dot(h_ref[:, :half], w2a[0], preferred_element_type=F32)
        out = out + jnp.dot(h_ref[:, half:], w2b[0], preferred_element_type=F32)
        obuf[slot] = out
        for r in range(tm):
            _row_copy(obuf.at[slot], r, y_hbm, dst_ref[0, 0, r], sems.at[slot]).start()

    @pl.when(i == n_steps - 1)
    def _():
        @pl.when(i >= 1)
        def _():
            drain(i - 1, 1 - slot)
        drain(i, slot)


def _moe_down(h_rows, plan, w2, n_tok):
    n_tiles = plan["n_tiles"]
    tm = MOE_TILE
    n_e, de, d = w2.shape
    wspec = lambda which: pl.BlockSpec((1, de, d), lambda i, lo, hi, nv: ((lo, hi)[which][i], 0, 0))
    grid_spec = pltpu.PrefetchScalarGridSpec(
        num_scalar_prefetch=3,
        grid=(n_tiles,),
        in_specs=[pl.BlockSpec((1, 1, tm), lambda i, *_: (i, 0, 0), memory_space=pltpu.SMEM),
                  pl.BlockSpec((tm, 2 * de), lambda i, *_: (i, 0)),
                  wspec(0), wspec(1)],
        out_specs=pl.BlockSpec(memory_space=pl.ANY),
        scratch_shapes=[pltpu.VMEM((2, tm, d), F32), pltpu.SemaphoreType.DMA((2,))],
    )
    return pl.pallas_call(
        _moe_down_kernel,
        grid_spec=grid_spec,
        out_shape=jax.ShapeDtypeStruct((n_tok + 2 * tm, d), F32),
        compiler_params=_cparams(1),
        name="moe_down",
    )(plan["e_lo"], plan["e_hi"], plan["n_valid"], plan["dst"], h_rows, w2, w2)


def _hier_moe(f_rows, logits, w1, w3, w2):
    n_tok = f_rows.shape[0]
    info, counts = _route(logits)
    plan = _dispatch_plan(info, counts, n_tok)
    h_rows = _moe_up(f_rows, plan, w1, w3)
    return _moe_down(h_rows, plan, w2, n_tok)


def _resid_kernel(x_ref, y_ref, mod_ref, o_ref):
    o_ref[0] = x_ref[0] + mod_ref[0, 0][5:6] * y_ref[...]


def _final_kernel(x_ref, y_ref, mod_ref, g_ref, o_ref):
    x = x_ref[0] + mod_ref[0, 0][5:6] * y_ref[...]
    ms = jnp.mean(x * x, axis=-1, keepdims=True)
    o_ref[0] = (x * lax.rsqrt(ms + EPS)) * g_ref[...]


def _gated_residual(x, y_rows, modvec, first_block, n_blocks, mod_row, g_final=None):
    b, rows, d = x.shape
    nblk = rows // ROW_BLOCK
    in_specs = [pl.BlockSpec((1, ROW_BLOCK, d), lambda bi, i: (bi, first_block + i, 0)),
                pl.BlockSpec((ROW_BLOCK, d), lambda bi, i: (bi * nblk + first_block + i, 0)),
                pl.BlockSpec((1, 1, 6, d), lambda bi, i: (bi, mod_row, 0, 0))]
    args = [x, y_rows, modvec]
    kern = _resid_kernel
    if g_final is not None:
        in_specs.append(_resident((1, d)))
        args.append(g_final)
        kern = _final_kernel
    return pl.pallas_call(
        kern,
        grid=(b, n_blocks),
        in_specs=in_specs,
        out_specs=pl.BlockSpec((1, ROW_BLOCK, d), lambda bi, i: (bi, i, 0)),
        out_shape=jax.ShapeDtypeStruct((b, n_blocks * ROW_BLOCK, d), F32),
        compiler_params=_cparams(2),
        name="gated_residual" if g_final is None else "final_norm",
    )(*args)


def _rope_tables(t, ctx_len):
    pos = jnp.arange(t)
    row = (pos // GRID_W).astype(F32)
    col = (pos % GRID_W).astype(F32)
    n_freq = HEAD_DIM // 4
    inv = ROPE_BASE ** (-jnp.arange(n_freq, dtype=F32) / n_freq)
    ang = jnp.concatenate([row[:, None] * inv, col[:, None] * inv], axis=-1)
    cos, sin = jnp.cos(ang), jnp.sin(ang)
    cosf = jnp.concatenate([cos, cos], axis=-1)
    sinf = jnp.concatenate([-sin, sin], axis=-1)
    cosf = jnp.concatenate([cosf, jnp.ones((ctx_len, HEAD_DIM), F32)], axis=0)
    sinf = jnp.concatenate([sinf, jnp.zeros((ctx_len, HEAD_DIM), F32)], axis=0)
    return cosf, sinf


def _retention_tables(decay_logit):
    lg_f = jax.nn.log_sigmoid(decay_logit[0].astype(F32))
    lg_b = jax.nn.log_sigmoid(decay_logit[1].astype(F32))
    n_heads = lg_f.shape[0]
    i = jnp.arange(CHUNK, dtype=F32)
    diff = i[:, None] - i[None, :]
    fwd = jnp.exp(jnp.where(diff >= 0, diff[None] * lg_f[:, None, None], -jnp.inf))
    bwd = jnp.exp(jnp.where(diff < 0, -diff[None] * lg_b[:, None, None], -jnp.inf))
    rows = lambda v: jnp.broadcast_to(v[:, :, None], (n_heads, CHUNK, HEAD_DIM))
    return dict(
        dmat=fwd + bwd,
        qdf=rows(jnp.exp((i + 1.0)[None, :] * lg_f[:, None])),
        qdb=rows(jnp.exp((CHUNK - i)[None, :] * lg_b[:, None])),
        kdf=rows(jnp.exp((CHUNK - 1.0 - i)[None, :] * lg_f[:, None])),
        kdb=rows(jnp.exp(i[None, :] * lg_b[:, None])),
        cdf=jnp.broadcast_to(jnp.exp(CHUNK * lg_f)[:, None, None], (n_heads, 1, HEAD_DIM)),
        cdb=jnp.broadcast_to(jnp.exp(CHUNK * lg_b)[:, None, None], (n_heads, 1, HEAD_DIM)),
    )


def _router_weights(w_r1, b_r1, w_r2, b_r2):
    d = w_r1.shape[0]
    n_exp = N_GROUPS * EXPERTS_PER_GROUP
    w = jnp.concatenate([w_r1, w_r2.transpose(1, 0, 2).reshape(d, n_exp)], axis=1)
    b = jnp.concatenate([b_r1, b_r2.reshape(n_exp)])
    pad = LANES - w.shape[1]
    return jnp.pad(w, ((0, 0), (0, pad))).astype(BF16), jnp.pad(b, (0, pad)).reshape(1, LANES)


def kernel(x, c, ctx, c_ctx, w_mod, b_mod, g_mix, g_ffn, ab_w_in, ab_w_out, ret_decay_logit, sgu_w_s, sgu_b_s,
           cv_w_in, cv_conv_w, cv_conv_b, cv_w_out, moe_w_r1, moe_b_r1, moe_w_r2, moe_b_r2, moe_w1, moe_w3,
           moe_w2, g_final):
    b, t, d = x.shape
    ctx_len = ctx.shape[1]
    n_lat_blocks = t // ROW_BLOCK
    n_heads = ret_decay_logit.shape[2]

    cvec = jnp.zeros((8, d), F32).at[:b].set(c).at[b].set(c_ctx)
    mod = _mod_vectors(cvec, w_mod, b_mod)
    mod_lat = mod[:, :b].reshape(2, b, 1, 6, d)
    mod_ctx = jnp.broadcast_to(mod[:, b].reshape(2, 1, 1, 6, d), (2, b, 1, 6, d))
    modvec = jnp.concatenate([mod_lat, mod_ctx], axis=2)

    x_all = jnp.concatenate([x, ctx], axis=1)
    cosf, sinf = _rope_tables(t, ctx_len)
    tabs = _retention_tables(ret_decay_logit[0])
    z_all = _ab_in_proj(x_all, modvec[0], g_mix[0:1], ab_w_in[0].astype(BF16), cosf, sinf, n_lat_blocks)
    r_states = _ret_bwd_states(z_all, tabs["kdb"], tabs["cdb"], n_heads)
    b_s_rows = jnp.broadcast_to(sgu_b_s[0][:, :, None], sgu_b_s[0].shape + (HEAD_DIM,))
    w_r, b_r = _router_weights(moe_w_r1[0], moe_b_r1[0], moe_w_r2[0], moe_b_r2[0])
    x_mid, f0, lg0 = _ab_mix(z_all, r_states, x_all, modvec[0], ab_w_out[0].astype(BF16),
                             sgu_w_s[0].astype(BF16), b_s_rows, tabs, g_ffn[0:1], w_r, b_r, n_lat_blocks)
    n0 = b * (t + ctx_len)
    y0 = _hier_moe(f0.reshape(n0, d), lg0.reshape(n0, LANES), moe_w1[0].astype(BF16),
                   moe_w3[0].astype(BF16), moe_w2[0].astype(BF16))
    ctx_out = _gated_residual(x_mid, y0, modvec[0], n_lat_blocks, ctx_len // ROW_BLOCK, 1)
    del ctx_out

    p = _cv_in_proj(x_mid, y0, modvec[0], modvec[1], g_mix[1:2], cv_w_in[0].astype(BF16), cv_conv_w[0],
                    cv_conv_b[0:1], n_lat_blocks)
    w_r, b_r = _router_weights(moe_w_r1[1], moe_b_r1[1], moe_w_r2[1], moe_b_r2[1])
    x2, f1, lg1 = _cv_out_proj(p, x_mid, y0, modvec[0], modvec[1], cv_w_out[0].astype(BF16), g_ffn[1:2],
                               w_r, b_r)
    y1 = _hier_moe(f1.reshape(b * t, d), lg1.reshape(b * t, LANES), moe_w1[1].astype(BF16),
                   moe_w3[1].astype(BF16), moe_w2[1].astype(BF16))
    return _gated_residual(x2, y1, modvec[1], 0, n_lat_blocks, 0, g_final.reshape(1, d))
```

```python
import functools

import jax
import jax.numpy as jnp
import numpy as np
from jax import lax
from jax.experimental import pallas as pl
from jax.experimental.pallas import tpu as pltpu

F32 = jnp.float32
BF16 = jnp.bfloat16

EPS = 1e-6
GRID_W = 64
HEAD_DIM = 128
CHUNK = 128
ROPE_BASE = 10000.0
N_GROUPS = 4
EXPERTS_PER_GROUP = 4
N_PAIRS = 6
N_CLASSES = N_GROUPS * N_PAIRS
PAIR_SLOT_A = (0, 2, 2, 3, 3, 3)
PAIR_SLOT_B = (1, 1, 0, 0, 1, 2)
PAIR_UP_FIRST = (0, 1, 0, 1, 0, 0)
LANES = 128
ROW_BLOCK = 256
MOE_TILE = 128
COL_CHUNK = 512
STATE_HEADS_PER_STEP = 4
CAST_ROWS = 256
VMEM_LIMIT = 56 * 1024 * 1024
GELU_C = float(np.float32(np.sqrt(2.0 / np.pi)))


def _cparams(n_axes):
    return pltpu.CompilerParams(dimension_semantics=("arbitrary",) * n_axes,
                                vmem_limit_bytes=VMEM_LIMIT)


def _resident(shape):
    zeros = (0,) * len(shape)
    return pl.BlockSpec(shape, lambda *_: zeros, pipeline_mode=pl.Buffered(1))


def _sigmoid(x):
    return 1.0 / (1.0 + jnp.exp(-x))


def _gelu_tanh(x):
    return x * (0.5 * (1.0 + jnp.tanh(GELU_C * (x + 0.044715 * (x * x * x)))))


def _modulate(x, g, shift, scale):
    ms = jnp.mean(x * x, axis=-1, keepdims=True)
    return (x * lax.rsqrt(ms + EPS) * g) * (1.0 + scale) + shift


def _standardise(y):
    mu = jnp.mean(y, axis=-1, keepdims=True)
    d = y - mu
    var = jnp.mean(d * d, axis=-1, keepdims=True)
    return d * lax.rsqrt(var + EPS)


def _mod_kernel(c_ref, w_ref, b_ref, o_ref):
    c = c_ref[...]
    s = (c * _sigmoid(c)).astype(BF16)
    o_ref[0] = jnp.dot(s, w_ref[0].astype(BF16), preferred_element_type=F32) + b_ref[0]


def _mod_vectors(cvec, w_mod, b_mod):
    depth, d, n = w_mod.shape
    bn = 1536
    return pl.pallas_call(
        _mod_kernel,
        grid=(depth, n // bn),
        in_specs=[pl.BlockSpec((8, d), lambda l, j: (0, 0)),
                  pl.BlockSpec((1, d, bn), lambda l, j: (l, 0, j)),
                  pl.BlockSpec((1, 1, bn), lambda l, j: (l, 0, j))],
        out_specs=pl.BlockSpec((1, 8, bn), lambda l, j: (l, 0, j)),
        out_shape=jax.ShapeDtypeStruct((depth, 8, n), F32),
        compiler_params=_cparams(2),
        name="mod_vectors",
    )(cvec, w_mod, b_mod.reshape(depth, 1, n))


def _ab_in_kernel(x_ref, mod_ref, g_ref, w_ref, cos_ref, sin_ref, z_ref, *, ret_width, qk_scale):
    mod = mod_ref[0, 0]
    a = _modulate(x_ref[0], g_ref[...], mod[0:1], mod[1:2]).astype(BF16)
    cosf = cos_ref[...]
    sinf = sin_ref[...]
    n_out = w_ref.shape[1]
    for j in range(n_out // COL_CHUNK):
        c0 = j * COL_CHUNK
        z = jnp.dot(a, w_ref[:, c0:c0 + COL_CHUNK], preferred_element_type=F32)
        if c0 < 2 * ret_width:
            parts = []
            for hh in range(COL_CHUNK // HEAD_DIM):
                zh = z[:, hh * HEAD_DIM:(hh + 1) * HEAD_DIM]
                zh = zh * cosf + pltpu.roll(zh, HEAD_DIM // 2, axis=1) * sinf
                if c0 < ret_width:
                    zh = zh * qk_scale
                parts.append(zh)
            z = jnp.concatenate(parts, axis=1)
        z_ref[0, :, c0:c0 + COL_CHUNK] = z.astype(BF16)


def _ab_in_proj(x_all, modvec, g_mix, w_in, cosf, sinf, n_lat_blocks):
    b, rows, d = x_all.shape
    n_out = w_in.shape[1]
    ret_width = n_out // 6
    kern = functools.partial(_ab_in_kernel, ret_width=ret_width, qk_scale=HEAD_DIM ** -0.5)
    return pl.pallas_call(
        kern,
        grid=(b, rows // ROW_BLOCK),
        in_specs=[pl.BlockSpec((1, ROW_BLOCK, d), lambda bi, i: (bi, i, 0)),
                  pl.BlockSpec((1, 1, 6, d), lambda bi, i: (bi, i // n_lat_blocks, 0, 0)),
                  _resident((1, d)),
                  _resident((d, n_out)),
                  pl.BlockSpec((ROW_BLOCK, HEAD_DIM), lambda bi, i: (i, 0)),
                  pl.BlockSpec((ROW_BLOCK, HEAD_DIM), lambda bi, i: (i, 0))],
        out_specs=pl.BlockSpec((1, ROW_BLOCK, n_out), lambda bi, i: (bi, i, 0)),
        out_shape=jax.ShapeDtypeStruct((b, rows, n_out), BF16),
        compiler_params=_cparams(2),
        name="ab_in_proj",
    )(x_all, modvec, g_mix, w_in, cosf, sinf)


def _ret_state_kernel(k_ref, v_ref, kdec_ref, cdec_ref, r_ref, s_scr, *, n_chunks):
    s_scr[...] = jnp.zeros_like(s_scr)

    def body(i, carry):
        c = n_chunks - 1 - i
        rows = pl.ds(pl.multiple_of(c * CHUNK, CHUNK), CHUNK)
        for hh in range(s_scr.shape[0]):
            cols = slice(hh * HEAD_DIM, (hh + 1) * HEAD_DIM)
            r_ref[0, c, hh] = s_scr[hh].astype(BF16)
            kd = (k_ref[0, rows, cols].astype(F32) * kdec_ref[hh]).astype(BF16)
            u = lax.dot_general(kd, v_ref[0, rows, cols], (((0,), (0,)), ((), ())),
                                preferred_element_type=F32)
            s_scr[hh] = s_scr[hh] * cdec_ref[hh] + u
        return carry

    lax.fori_loop(0, n_chunks, body, 0)


def _ret_bwd_states(z_all, kdec_b, cdec_b, n_heads):
    b, rows, _ = z_all.shape
    n_chunks = rows // CHUNK
    hps = STATE_HEADS_PER_STEP
    wblk = hps * HEAD_DIM
    kern = functools.partial(_ret_state_kernel, n_chunks=n_chunks)
    return pl.pallas_call(
        kern,
        grid=(b, n_heads // hps),
        in_specs=[pl.BlockSpec((1, rows, wblk), lambda bi, g: (bi, 0, n_heads // hps + g)),
                  pl.BlockSpec((1, rows, wblk), lambda bi, g: (bi, 0, 2 * (n_heads // hps) + g)),
                  pl.BlockSpec((hps, CHUNK, HEAD_DIM), lambda bi, g: (g, 0, 0)),
                  pl.BlockSpec((hps, 1, HEAD_DIM), lambda bi, g: (g, 0, 0))],
        out_specs=pl.BlockSpec((1, n_chunks, hps, HEAD_DIM, HEAD_DIM), lambda bi, g: (bi, 0, g, 0, 0)),
        out_shape=jax.ShapeDtypeStruct((b, n_chunks, n_heads, HEAD_DIM, HEAD_DIM), BF16),
        scratch_shapes=[pltpu.VMEM((hps, HEAD_DIM, HEAD_DIM), F32)],
        compiler_params=_cparams(2),
        name="ret_bwd_states",
    )(z_all, z_all, kdec_b, cdec_b)


def _residual_ffn_in(x, m, mod, gffn, wr_ref, br_ref, xo_ref, f_ref, lg_ref):
    xn = x + mod[2:3] * m
    xo_ref[0] = xn
    f = _modulate(xn, gffn, mod[3:4], mod[4:5])
    f_ref[0] = f
    lg_ref[0] = jnp.dot(f.astype(BF16), wr_ref[...], preferred_element_type=F32) + br_ref[...]


def _ab_mix_kernel(z_ref, r_ref, x_ref, mod_ref, wout_ref, ws_ref, bs_ref, dmat_ref, qdf_ref, qdb_ref,
                   kdf_ref, cdf_ref, gffn_ref, wr_ref, br_ref, xo_ref, f_ref, lg_ref, s_scr, rs_scr,
                   *, n_heads, n_sgu):
    @pl.when(pl.program_id(1) == 0)
    def _():
        s_scr[...] = jnp.zeros_like(s_scr)

    ret_w = n_heads * HEAD_DIM
    sgu_w = n_sgu * HEAD_DIM
    nt = (((1,), (1,)), ((), ()))
    tn = (((0,), (0,)), ((), ()))
    for c in range(ROW_BLOCK // CHUNK):
        r0 = c * CHUNK
        for h in range(n_heads):
            c0 = h * HEAD_DIM
            q = z_ref[0, r0:r0 + CHUNK, c0:c0 + HEAD_DIM]
            k = z_ref[0, r0:r0 + CHUNK, ret_w + c0:ret_w + c0 + HEAD_DIM]
            v = z_ref[0, r0:r0 + CHUNK, 2 * ret_w + c0:2 * ret_w + c0 + HEAD_DIM]
            g = z_ref[0, r0:r0 + CHUNK, 3 * ret_w + c0:3 * ret_w + c0 + HEAD_DIM].astype(F32)
            sc = lax.dot_general(q, k, nt, preferred_element_type=F32) * dmat_ref[h]
            y = jnp.dot(sc.astype(BF16), v, preferred_element_type=F32)
            y = y + jnp.dot(q, s_scr[h].astype(BF16), preferred_element_type=F32) * qdf_ref[h]
            y = y + jnp.dot(q, r_ref[0, c, h], preferred_element_type=F32) * qdb_ref[h]
            r = (g * _sigmoid(g)) * _standardise(y)
            rs_scr[r0:r0 + CHUNK, c0:c0 + HEAD_DIM] = r.astype(BF16)
            kd = (k.astype(F32) * kdf_ref[h]).astype(BF16)
            s_scr[h] = s_scr[h] * cdf_ref[h] + lax.dot_general(kd, v, tn, preferred_element_type=F32)
        for gi in range(n_sgu):
            c0 = 4 * ret_w + gi * HEAD_DIM
            u = _gelu_tanh(z_ref[0, r0:r0 + CHUNK, c0:c0 + HEAD_DIM].astype(F32))
            vv = _gelu_tanh(z_ref[0, r0:r0 + CHUNK, sgu_w + c0:sgu_w + c0 + HEAD_DIM].astype(F32))
            vn = _standardise(vv).astype(BF16)
            sg = jnp.dot(ws_ref[gi], vn, preferred_element_type=F32) + bs_ref[gi]
            rs_scr[r0:r0 + CHUNK, ret_w + gi * HEAD_DIM:ret_w + (gi + 1) * HEAD_DIM] = (u * sg).astype(BF16)
    m = jnp.dot(rs_scr[...], wout_ref[...], preferred_element_type=F32)
    _residual_ffn_in(x_ref[0], m, mod_ref[0, 0], gffn_ref[...], wr_ref, br_ref, xo_ref, f_ref, lg_ref)


def _ab_mix(z_all, r_states, x_all, modvec, w_out, w_s, b_s_rows, tabs, g_ffn, w_r, b_r, n_lat_blocks):
    b, rows, d = x_all.shape
    n_out = z_all.shape[2]
    n_heads = r_states.shape[2]
    n_sgu = w_s.shape[0]
    nblk = rows // ROW_BLOCK
    cpb = ROW_BLOCK // CHUNK
    blk = lambda i: (i + n_lat_blocks) % nblk
    kern = functools.partial(_ab_mix_kernel, n_heads=n_heads, n_sgu=n_sgu)
    tab_spec = _resident((n_heads, CHUNK, HEAD_DIM))
    row_out = lambda w, dt: (pl.BlockSpec((1, ROW_BLOCK, w), lambda bi, i: (bi, blk(i), 0)),
                             jax.ShapeDtypeStruct((b, rows, w), dt))
    outs = [row_out(d, F32), row_out(d, F32), row_out(LANES, F32)]
    return pl.pallas_call(
        kern,
        grid=(b, nblk),
        in_specs=[pl.BlockSpec((1, ROW_BLOCK, n_out), lambda bi, i: (bi, blk(i), 0)),
                  pl.BlockSpec((1, cpb, n_heads, HEAD_DIM, HEAD_DIM), lambda bi, i: (bi, blk(i), 0, 0, 0)),
                  pl.BlockSpec((1, ROW_BLOCK, d), lambda bi, i: (bi, blk(i), 0)),
                  pl.BlockSpec((1, 1, 6, d), lambda bi, i: (bi, blk(i) // n_lat_blocks, 0, 0)),
                  _resident(w_out.shape),
                  _resident(w_s.shape),
                  _resident(b_s_rows.shape),
                  tab_spec, tab_spec, tab_spec, tab_spec,
                  _resident((n_heads, 1, HEAD_DIM)),
                  _resident((1, d)),
                  _resident(w_r.shape),
                  _resident((1, LANES))],
        out_specs=[o[0] for o in outs],
        out_shape=[o[1] for o in outs],
        scratch_shapes=[pltpu.VMEM((n_heads, HEAD_DIM, HEAD_DIM), F32),
                        pltpu.VMEM((ROW_BLOCK, d), BF16)],
        compiler_params=_cparams(2),
        name="ab_mix",
    )(z_all, r_states, x_all, modvec, w_out, w_s, b_s_rows, tabs["dmat"], tabs["qdf"], tabs["qdb"],
      tabs["kdf"], tabs["cdf"], g_ffn, w_r, b_r)


def _cv_in_kernel(x_ref, y_ref, mod0_ref, mod_ref, g_ref, w_ref, cw_ref, cb_ref, p_ref):
    mod = mod_ref[0, 0]
    x1 = x_ref[0] + mod0_ref[0, 0][5:6] * y_ref[...]
    a = _modulate(x1, g_ref[...], mod[0:1], mod[1:2]).astype(BF16)
    bm = a.shape[0]
    width = p_ref.shape[2]
    col = lax.broadcasted_iota(jnp.int32, (bm, COL_CHUNK), 0) % GRID_W
    for j in range(width // COL_CHUNK):
        c0 = j * COL_CHUNK
        gate_b = jnp.dot(a, w_ref[:, c0:c0 + COL_CHUNK], preferred_element_type=F32)
        gate_c = jnp.dot(a, w_ref[:, width + c0:width + c0 + COL_CHUNK], preferred_element_type=F32)
        hv = jnp.dot(a, w_ref[:, 2 * width + c0:2 * width + c0 + COL_CHUNK], preferred_element_type=F32)
        xg = gate_c * hv
        left = jnp.where(col == 0, 0.0, pltpu.roll(xg, 1, axis=0))
        right = jnp.where(col == GRID_W - 1, 0.0, pltpu.roll(xg, bm - 1, axis=0))
        y = cb_ref[:, c0:c0 + COL_CHUNK] + left * cw_ref[0:1, c0:c0 + COL_CHUNK]
        y = y + xg * cw_ref[1:2, c0:c0 + COL_CHUNK]
        y = y + right * cw_ref[2:3, c0:c0 + COL_CHUNK]
        p_ref[0, :, c0:c0 + COL_CHUNK] = (gate_b * y).astype(BF16)


def _cv_in_proj(x_mid, y_rows, modvec_prev, modvec, g_mix, w_in, conv_w, conv_b, n_lat_blocks):
    b, rows, d = x_mid.shape
    nblk = rows // ROW_BLOCK
    width = conv_w.shape[1]
    t = n_lat_blocks * ROW_BLOCK
    lat_mod = pl.BlockSpec((1, 1, 6, d), lambda bi, i: (bi, 0, 0, 0))
    return pl.pallas_call(
        _cv_in_kernel,
        grid=(b, n_lat_blocks),
        in_specs=[pl.BlockSpec((1, ROW_BLOCK, d), lambda bi, i: (bi, i, 0)),
                  pl.BlockSpec((ROW_BLOCK, d), lambda bi, i: (bi * nblk + i, 0)),
                  lat_mod, lat_mod,
                  _resident((1, d)),
                  _resident(w_in.shape),
                  _resident(conv_w.shape),
                  _resident((1, width))],
        out_specs=pl.BlockSpec((1, ROW_BLOCK, width), lambda bi, i: (bi, i, 0)),
        out_shape=jax.ShapeDtypeStruct((b, t, width), BF16),
        compiler_params=_cparams(2),
        name="cv_in_proj",
    )(x_mid, y_rows, modvec_prev, modvec, g_mix, w_in, conv_w, conv_b)


def _cv_out_kernel(p_ref, x_ref, y_ref, mod0_ref, mod_ref, wout_ref, gffn_ref, wr_ref, br_ref,
                   xo_ref, f_ref, lg_ref):
    x1 = x_ref[0] + mod0_ref[0, 0][5:6] * y_ref[...]
    m = jnp.dot(p_ref[0], wout_ref[...], preferred_element_type=F32)
    _residual_ffn_in(x1, m, mod_ref[0, 0], gffn_ref[...], wr_ref, br_ref, xo_ref, f_ref, lg_ref)


def _cv_out_proj(p, x_mid, y_rows, modvec_prev, modvec, w_out, g_ffn, w_r, b_r):
    b, t, width = p.shape
    d = w_out.shape[1]
    nblk = x_mid.shape[1] // ROW_BLOCK
    row_out = lambda w: (pl.BlockSpec((1, ROW_BLOCK, w), lambda bi, i: (bi, i, 0)),
                         jax.ShapeDtypeStruct((b, t, w), F32))
    outs = [row_out(d), row_out(d), row_out(LANES)]
    lat_mod = pl.BlockSpec((1, 1, 6, d), lambda bi, i: (bi, 0, 0, 0))
    return pl.pallas_call(
        _cv_out_kernel,
        grid=(b, t // ROW_BLOCK),
        in_specs=[pl.BlockSpec((1, ROW_BLOCK, width), lambda bi, i: (bi, i, 0)),
                  pl.BlockSpec((1, ROW_BLOCK, d), lambda bi, i: (bi, i, 0)),
                  pl.BlockSpec((ROW_BLOCK, d), lambda bi, i: (bi * nblk + i, 0)),
                  lat_mod, lat_mod,
                  _resident(w_out.shape),
                  _resident((1, d)),
                  _resident(w_r.shape),
                  _resident((1, LANES))],
        out_specs=[o[0] for o in outs],
        out_shape=[o[1] for o in outs],
        compiler_params=_cparams(2),
        name="cv_out_proj",
    )(p, x_mid, y_rows, modvec_prev, modvec, w_out, g_ffn, w_r, b_r)


def _route_kernel(lg_ref, info_ref, cnt_ref, cnt_scr):
    @pl.when(pl.program_id(0) == 0)
    def _():
        cnt_scr[...] = jnp.zeros_like(cnt_scr)

    lg = lg_ref[...]
    bm = lg.shape[0]
    lane = lax.broadcasted_iota(jnp.int32, lg.shape, 1).astype(F32)
    neg = -jnp.inf

    def first_max(vals):
        mx = jnp.max(vals, axis=1, keepdims=True)
        idx = jnp.min(jnp.where(vals == mx, lane, float(LANES)), axis=1, keepdims=True)
        return mx, idx

    is_grp = lane < N_GROUPS
    m1, grp = first_max(jnp.where(is_grp, lg, neg))
    den = jnp.sum(jnp.where(is_grp, jnp.exp(lg - m1), 0.0), axis=1, keepdims=True)
    p_grp = 1.0 / den
    base = N_GROUPS + EXPERTS_PER_GROUP * grp
    in_grp = (lane >= base) & (lane < base + EXPERTS_PER_GROUP)
    l2 = jnp.where(in_grp, lg, neg)
    v1, i1 = first_max(l2)
    v2, i2 = first_max(jnp.where(lane == i1, neg, l2))
    e = jnp.exp(v2 - v1)
    w_top1 = p_grp * (1.0 / (1.0 + e))
    w_top2 = p_grp * (e / (1.0 + e))
    first_lo = i1 < i2
    lo = jnp.where(first_lo, i1, i2) - base
    hi = jnp.where(first_lo, i2, i1) - base
    w_lo = jnp.where(first_lo, w_top1, w_top2)
    w_hi = jnp.where(first_lo, w_top2, w_top1)
    pair = jnp.where(lo == 0.0, jnp.where(hi == 1.0, 0.0, hi),
                     jnp.where(lo == 1.0, jnp.where(hi == 2.0, 1.0, 4.0), 5.0))
    w_a = jnp.where(pair == 0.0, w_lo, w_hi)
    w_b = jnp.where(pair == 0.0, w_hi, w_lo)
    cls = grp * N_PAIRS + pair

    onehot = (lane == cls).astype(F32)
    tri = (lax.broadcasted_iota(jnp.int32, (bm, bm), 0) > lax.broadcasted_iota(jnp.int32, (bm, bm), 1))
    before = jnp.dot(tri.astype(BF16), onehot.astype(BF16), preferred_element_type=F32)
    rank = jnp.sum(onehot * (before + cnt_scr[...]), axis=1, keepdims=True)
    cnt_scr[...] = cnt_scr[...] + jnp.sum(onehot, axis=0, keepdims=True)
    cnt_ref[...] = cnt_scr[...]
    info = jnp.where(lane == 0, cls,
                     jnp.where(lane == 1, rank,
                               jnp.where(lane == 2, w_a, jnp.where(lane == 3, w_b, 0.0))))
    info_ref[...] = info


def _route(logits):
    n = logits.shape[0]
    return pl.pallas_call(
        _route_kernel,
        grid=(n // ROW_BLOCK,),
        in_specs=[pl.BlockSpec((ROW_BLOCK, LANES), lambda i: (i, 0))],
        out_specs=[pl.BlockSpec((ROW_BLOCK, LANES), lambda i: (i, 0)),
                   pl.BlockSpec((1, LANES), lambda i: (0, 0))],
        out_shape=[jax.ShapeDtypeStruct((n, LANES), F32), jax.ShapeDtypeStruct((1, LANES), F32)],
        scratch_shapes=[pltpu.VMEM((1, LANES), F32)],
        compiler_params=_cparams(1),
        name="moe_route",
    )(logits)


def _dispatch_plan(info, counts, n_tok, expert_base):
    tm = MOE_TILE
    n_tiles = n_tok // tm + N_CLASSES
    n_rows = n_tiles * tm
    i32 = jnp.int32
    slot_a = jnp.asarray(PAIR_SLOT_A, i32)
    slot_b = jnp.asarray(PAIR_SLOT_B, i32)
    up_first = jnp.asarray(PAIR_UP_FIRST, i32)
    cls = info[:, 0].astype(i32)
    rank = info[:, 1].astype(i32)
    cnt = counts[0, :N_CLASSES].astype(i32)
    tiles_c = (cnt + tm - 1) // tm
    tile_end = jnp.cumsum(tiles_c)
    tile_start = tile_end - tiles_c
    pos = tile_start[cls] * tm + rank
    tok = jnp.arange(n_tok, dtype=F32)[:, None]
    per_row = jnp.zeros((n_rows, 3), F32).at[pos].set(jnp.concatenate([tok, info[:, 2:4]], axis=1))
    src = per_row[:, 0].astype(i32).reshape(n_tiles, tm)
    w_rows = per_row[:, 1:3]
    n_used = tile_end[-1]

    def class_of(step, ends):
        return jnp.minimum(jnp.sum((ends[None, :] <= step[:, None]).astype(i32), axis=1), N_CLASSES - 1)

    tile = jnp.arange(n_tiles, dtype=i32)
    tcls = class_of(jnp.minimum(tile, n_used - 1), tile_end)
    e_a = expert_base + (tcls // N_PAIRS) * EXPERTS_PER_GROUP + slot_a[tcls % N_PAIRS]
    e_b = expert_base + (tcls // N_PAIRS) * EXPERTS_PER_GROUP + slot_b[tcls % N_PAIRS]
    n_valid = jnp.clip(cnt[tcls] - (tile - tile_start[tcls]) * tm, 0, tm)
    n_valid = jnp.where(tile < n_used, n_valid, 0).astype(i32)
    row = jnp.arange(tm, dtype=i32)[None, :]
    dump = n_tok + (tile % 2)[:, None] * tm + row
    dst = jnp.where(row < n_valid[:, None], src, dump)

    n_items = 2 * n_used
    item = jnp.arange(2 * n_tiles, dtype=i32)
    real = item < n_items
    qc = jnp.minimum(item, n_items - 1)
    icls = class_of(qc, 2 * tile_end)
    local = qc - 2 * tile_start[icls]
    per = jnp.maximum(tiles_c[icls], 1)
    pair = icls % N_PAIRS
    slot = jnp.where(local // per == 0, up_first[pair], 1 - up_first[pair])
    it_exp = (expert_base + (icls // N_PAIRS) * EXPERTS_PER_GROUP
              + jnp.where(slot == 0, slot_a[pair], slot_b[pair]))
    spare = item - n_items
    it_tile = jnp.where(real, tile_start[icls] + local % per, n_used + spare // 2)
    it_slot = jnp.where(real, slot, spare % 2)

    return dict(src=src.reshape(n_tiles, 1, tm), dst=dst.reshape(n_tiles, 1, tm), w_rows=w_rows,
                n_valid=n_valid, n_tiles=n_tiles, n_items=n_items.reshape(1).astype(i32),
                it_tile=it_tile.astype(i32), it_slot=it_slot.astype(i32), it_exp=it_exp.astype(i32),
                up_runs=_weight_runs(it_exp, n_items), a_runs=_weight_runs(e_a, n_used),
                b_runs=_weight_runs(e_b, n_used), e_a=e_a.astype(i32), e_b=e_b.astype(i32))


def _weight_runs(expert, n_real):
    i32 = jnp.int32
    n = expert.shape[0]
    idx = jnp.arange(n, dtype=i32)
    prev = jnp.concatenate([jnp.full((1,), -1, i32), expert[:-1].astype(i32)])
    first = ((expert != prev) & (idx < n_real)).astype(i32)
    parity = (jnp.cumsum(first) - 1) % 2
    later = (idx[None, :] > idx[:, None]) & (first[None, :] == 1)
    nxt_idx = jnp.min(jnp.where(later, idx[None, :], n), axis=1)
    nxt = jnp.where(nxt_idx < n, expert[jnp.minimum(nxt_idx, n - 1)], -1)
    return first, parity.astype(i32), nxt.astype(i32)


def _row_copy(src_hbm, src_row, dst_buf, dst_row, sem):
    return pltpu.make_async_copy(src_hbm.at[pl.ds(src_row, 1), :], dst_buf.at[pl.ds(dst_row, 1), :], sem)


def _cast_rows(src, dst):
    def body(c, carry):
        rows = pl.ds(pl.multiple_of(c * CAST_ROWS, CAST_ROWS), CAST_ROWS)
        dst[rows, :] = src[rows, :].astype(BF16)
        return carry
    lax.fori_loop(0, src.shape[0] // CAST_ROWS, body, 0)


def _moe_up_kernel(tile_ref, slot_ref, exp_ref, first_ref, par_ref, nxt_ref, ni_ref,
                   src_ref, nsrc_ref, f_hbm, ws_ref, w1_hbm, w3_hbm, h_ref,
                   xbuf, xsem, wf32, wbf, wsem):
    q = pl.program_id(0)
    n_items = ni_ref[0]
    tm = xbuf.shape[1]
    xs = q % 2

    def fetch(expert, buf):
        pltpu.make_async_copy(w1_hbm.at[expert], wf32.at[buf, 0], wsem.at[buf]).start()
        pltpu.make_async_copy(w3_hbm.at[expert], wf32.at[buf, 1], wsem.at[buf]).start()

    def gather(idx_ref, s):
        for r in range(tm):
            _row_copy(f_hbm, idx_ref[0, 0, r], xbuf.at[s], r, xsem.at[s]).start()

    @pl.when(q == 0)
    def _():
        fetch(exp_ref[0], 0)
        gather(src_ref, 0)

    @pl.when(q < n_items)
    def _():
        buf = par_ref[q]

        @pl.when(first_ref[q] == 1)
        def _():
            for m in range(2):
                pltpu.make_async_copy(w1_hbm.at[0], wf32.at[buf, m], wsem.at[buf]).wait()

            @pl.when(nxt_ref[q] >= 0)
            def _():
                fetch(nxt_ref[q], 1 - buf)
            for m in range(2):
                _cast_rows(wf32.at[buf, m], wbf.at[m])

        @pl.when(q + 1 < n_items)
        def _():
            gather(nsrc_ref, 1 - xs)

        pltpu.make_async_copy(f_hbm.at[pl.ds(0, tm), :], xbuf.at[xs], xsem.at[xs]).wait()
        x = xbuf[xs].astype(BF16)
        up = jnp.dot(x, wbf[0], preferred_element_type=F32)
        gate = jnp.dot(x, wbf[1], preferred_element_type=F32)
        ws = ws_ref[...]
        w_col = jnp.where(slot_ref[q] == 0, ws[:, 0:1], ws[:, 1:2])
        h_ref[...] = ((up * _sigmoid(up)) * gate * w_col).astype(BF16)

    @pl.when(q >= n_items)
    def _():
        h_ref[...] = jnp.zeros_like(h_ref)


def _moe_up(f_rows, plan, w1, w3):
    n_tiles = plan["n_tiles"]
    tm = MOE_TILE
    d = f_rows.shape[1]
    n_e, _, de = w1.shape
    n_items = 2 * n_tiles
    first, parity, nxt = plan["up_runs"]
    tile_of = lambda q, tile_ref: tile_ref[jnp.minimum(q, n_items - 1)]
    grid_spec = pltpu.PrefetchScalarGridSpec(
        num_scalar_prefetch=7,
        grid=(n_items,),
        in_specs=[pl.BlockSpec((1, 1, tm), lambda q, t, *_: (t[q], 0, 0), memory_space=pltpu.SMEM),
                  pl.BlockSpec((1, 1, tm), lambda q, t, *_: (tile_of(q + 1, t), 0, 0),
                               memory_space=pltpu.SMEM),
                  pl.BlockSpec(memory_space=pl.ANY),
                  pl.BlockSpec((tm, 2), lambda q, t, *_: (t[q], 0)),
                  pl.BlockSpec(memory_space=pl.ANY),
                  pl.BlockSpec(memory_space=pl.ANY)],
        out_specs=pl.BlockSpec((tm, de), lambda q, t, s, *_: (t[q], s[q])),
        scratch_shapes=[pltpu.VMEM((2, tm, d), F32), pltpu.SemaphoreType.DMA((2,)),
                        pltpu.VMEM((2, 2, d, de), F32), pltpu.VMEM((2, d, de), BF16),
                        pltpu.SemaphoreType.DMA((2,))],
    )
    return pl.pallas_call(
        _moe_up_kernel,
        grid_spec=grid_spec,
        out_shape=jax.ShapeDtypeStruct((n_tiles * tm, 2 * de), BF16),
        compiler_params=_cparams(1),
        name="moe_up",
    )(plan["it_tile"], plan["it_slot"], plan["it_exp"], first, parity, nxt, plan["n_items"],
      plan["src"], plan["src"], f_rows, plan["w_rows"], w1, w3)


def _moe_down_kernel(nv_ref, ea_ref, fa_ref, pa_ref, na_ref, eb_ref, fb_ref, pb_ref, nb_ref,
                     dst_ref, h_ref, w2_hbm, y_hbm, obuf, sems, wf32, wbf, wsem):
    i = pl.program_id(0)
    n_steps = pl.num_programs(0)
    slot = i % 2
    half = wbf.shape[1]
    tm = obuf.shape[1]
    runs = ((ea_ref, fa_ref, pa_ref, na_ref), (eb_ref, fb_ref, pb_ref, nb_ref))

    def fetch(s, expert, buf):
        pltpu.make_async_copy(w2_hbm.at[expert], wf32.at[s, buf], wsem.at[s, buf]).start()

    def drain(tile, s):
        @pl.when(nv_ref[tile] > 0)
        def _():
            pltpu.make_async_copy(obuf.at[s], y_hbm.at[pl.ds(0, tm), :], sems.at[s]).wait()

    @pl.when(i == 0)
    def _():
        n_tok = y_hbm.shape[0] - 2 * tm
        obuf[...] = jnp.zeros_like(obuf)
        for s in range(2):
            fill = pltpu.make_async_copy(obuf.at[s], y_hbm.at[pl.ds(n_tok + s * tm, tm), :], sems.at[s])
            fill.start()
            fill.wait()
        for s, (e_ref, _, _, _) in enumerate(runs):
            fetch(s, e_ref[0], 0)

    @pl.when(i >= 2)
    def _():
        drain(i - 2, slot)

    @pl.when(nv_ref[i] > 0)
    def _():
        for s, (_, f_ref, p_ref, n_ref) in enumerate(runs):
            @pl.when(f_ref[i] == 1)
            def _():
                buf = p_ref[i]
                pltpu.make_async_copy(w2_hbm.at[0], wf32.at[s, buf], wsem.at[s, buf]).wait()

                @pl.when(n_ref[i] >= 0)
                def _():
                    fetch(s, n_ref[i], 1 - buf)
                _cast_rows(wf32.at[s, buf], wbf.at[s])

        out = jnp.dot(h_ref[:, :half], wbf[0], preferred_element_type=F32)
        out = out + jnp.dot(h_ref[:, half:], wbf[1], preferred_element_type=F32)
        obuf[slot] = out
        for r in range(tm):
            _row_copy(obuf.at[slot], r, y_hbm, dst_ref[0, 0, r], sems.at[slot]).start()

    @pl.when(i == n_steps - 1)
    def _():
        @pl.when(i >= 1)
        def _():
            drain(i - 1, 1 - slot)
        drain(i, slot)


def _moe_down(h_rows, plan, w2, n_tok):
    n_tiles = plan["n_tiles"]
    tm = MOE_TILE
    n_e, de, d = w2.shape
    grid_spec = pltpu.PrefetchScalarGridSpec(
        num_scalar_prefetch=9,
        grid=(n_tiles,),
        in_specs=[pl.BlockSpec((1, 1, tm), lambda i, *_: (i, 0, 0), memory_space=pltpu.SMEM),
                  pl.BlockSpec((tm, 2 * de), lambda i, *_: (i, 0)),
                  pl.BlockSpec(memory_space=pl.ANY)],
        out_specs=pl.BlockSpec(memory_space=pl.ANY),
        scratch_shapes=[pltpu.VMEM((2, tm, d), F32), pltpu.SemaphoreType.DMA((2,)),
                        pltpu.VMEM((2, 2, de, d), F32), pltpu.VMEM((2, de, d), BF16),
                        pltpu.SemaphoreType.DMA((2, 2))],
    )
    return pl.pallas_call(
        _moe_down_kernel,
        grid_spec=grid_spec,
        out_shape=jax.ShapeDtypeStruct((n_tok + 2 * tm, d), F32),
        compiler_params=_cparams(1),
        name="moe_down",
    )(plan["n_valid"], plan["e_a"], *plan["a_runs"], plan["e_b"], *plan["b_runs"], plan["dst"], h_rows, w2)


def _hier_moe(f_rows, logits, w1, w3, w2, layer):
    n_tok = f_rows.shape[0]
    n_layers, n_exp = w1.shape[:2]
    merged = lambda w: w.reshape((n_layers * n_exp,) + w.shape[2:])
    info, counts = _route(logits)
    plan = _dispatch_plan(info, counts, n_tok, layer * n_exp)
    h_rows = _moe_up(f_rows, plan, merged(w1), merged(w3))
    return _moe_down(h_rows, plan, merged(w2), n_tok)


def _resid_kernel(x_ref, y_ref, mod_ref, o_ref):
    o_ref[0] = x_ref[0] + mod_ref[0, 0][5:6] * y_ref[...]


def _final_kernel(x_ref, y_ref, mod_ref, g_ref, o_ref):
    x = x_ref[0] + mod_ref[0, 0][5:6] * y_ref[...]
    ms = jnp.mean(x * x, axis=-1, keepdims=True)
    o_ref[0] = (x * lax.rsqrt(ms + EPS)) * g_ref[...]


def _gated_residual(x, y_rows, modvec, first_block, n_blocks, mod_row, g_final=None):
    b, rows, d = x.shape
    nblk = rows // ROW_BLOCK
    in_specs = [pl.BlockSpec((1, ROW_BLOCK, d), lambda bi, i: (bi, first_block + i, 0)),
                pl.BlockSpec((ROW_BLOCK, d), lambda bi, i: (bi * nblk + first_block + i, 0)),
                pl.BlockSpec((1, 1, 6, d), lambda bi, i: (bi, mod_row, 0, 0))]
    args = [x, y_rows, modvec]
    kern = _resid_kernel
    if g_final is not None:
        in_specs.append(_resident((1, d)))
        args.append(g_final)
        kern = _final_kernel
    return pl.pallas_call(
        kern,
        grid=(b, n_blocks),
        in_specs=in_specs,
        out_specs=pl.BlockSpec((1, ROW_BLOCK, d), lambda bi, i: (bi, i, 0)),
        out_shape=jax.ShapeDtypeStruct((b, n_blocks * ROW_BLOCK, d), F32),
        compiler_params=_cparams(2),
        name="gated_residual" if g_final is None else "final_norm",
    )(*args)


def _rope_tables(t, ctx_len):
    pos = jnp.arange(t)
    row = (pos // GRID_W).astype(F32)
    col = (pos % GRID_W).astype(F32)
    n_freq = HEAD_DIM // 4
    inv = ROPE_BASE ** (-jnp.arange(n_freq, dtype=F32) / n_freq)
    ang = jnp.concatenate([row[:, None] * inv, col[:, None] * inv], axis=-1)
    cos, sin = jnp.cos(ang), jnp.sin(ang)
    cosf = jnp.concatenate([cos, cos], axis=-1)
    sinf = jnp.concatenate([-sin, sin], axis=-1)
    cosf = jnp.concatenate([cosf, jnp.ones((ctx_len, HEAD_DIM), F32)], axis=0)
    sinf = jnp.concatenate([sinf, jnp.zeros((ctx_len, HEAD_DIM), F32)], axis=0)
    return cosf, sinf


def _retention_tables(decay_logit):
    lg_f = jax.nn.log_sigmoid(decay_logit[0].astype(F32))
    lg_b = jax.nn.log_sigmoid(decay_logit[1].astype(F32))
    n_heads = lg_f.shape[0]
    i = jnp.arange(CHUNK, dtype=F32)
    diff = i[:, None] - i[None, :]
    fwd = jnp.exp(jnp.where(diff >= 0, diff[None] * lg_f[:, None, None], -jnp.inf))
    bwd = jnp.exp(jnp.where(diff < 0, -diff[None] * lg_b[:, None, None], -jnp.inf))
    rows = lambda v: jnp.broadcast_to(v[:, :, None], (n_heads, CHUNK, HEAD_DIM))
    return dict(
        dmat=fwd + bwd,
        qdf=rows(jnp.exp((i + 1.0)[None, :] * lg_f[:, None])),
        qdb=rows(jnp.exp((CHUNK - i)[None, :] * lg_b[:, None])),
        kdf=rows(jnp.exp((CHUNK - 1.0 - i)[None, :] * lg_f[:, None])),
        kdb=rows(jnp.exp(i[None, :] * lg_b[:, None])),
        cdf=jnp.broadcast_to(jnp.exp(CHUNK * lg_f)[:, None, None], (n_heads, 1, HEAD_DIM)),
        cdb=jnp.broadcast_to(jnp.exp(CHUNK * lg_b)[:, None, None], (n_heads, 1, HEAD_DIM)),
    )


def _router_weights(w_r1, b_r1, w_r2, b_r2):
    d = w_r1.shape[0]
    n_exp = N_GROUPS * EXPERTS_PER_GROUP
    w = jnp.concatenate([w_r1, w_r2.transpose(1, 0, 2).reshape(d, n_exp)], axis=1)
    b = jnp.concatenate([b_r1, b_r2.reshape(n_exp)])
    pad = LANES - w.shape[1]
    return jnp.pad(w, ((0, 0), (0, pad))).astype(BF16), jnp.pad(b, (0, pad)).reshape(1, LANES)


def kernel(x, c, ctx, c_ctx, w_mod, b_mod, g_mix, g_ffn, ab_w_in, ab_w_out, ret_decay_logit, sgu_w_s, sgu_b_s,
           cv_w_in, cv_conv_w, cv_conv_b, cv_w_out, moe_w_r1, moe_b_r1, moe_w_r2, moe_b_r2, moe_w1, moe_w3,
           moe_w2, g_final):
    b, t, d = x.shape
    ctx_len = ctx.shape[1]
    n_lat_blocks = t // ROW_BLOCK
    n_heads = ret_decay_logit.shape[2]

    cvec = jnp.zeros((8, d), F32).at[:b].set(c).at[b].set(c_ctx)
    mod = _mod_vectors(cvec, w_mod, b_mod)
    mod_lat = mod[:, :b].reshape(2, b, 1, 6, d)
    mod_ctx = jnp.broadcast_to(mod[:, b].reshape(2, 1, 1, 6, d), (2, b, 1, 6, d))
    modvec = jnp.concatenate([mod_lat, mod_ctx], axis=2)

    x_all = jnp.concatenate([x, ctx], axis=1)
    cosf, sinf = _rope_tables(t, ctx_len)
    tabs = _retention_tables(ret_decay_logit[0])
    z_all = _ab_in_proj(x_all, modvec[0], g_mix[0:1], ab_w_in[0].astype(BF16), cosf, sinf, n_lat_blocks)
    r_states = _ret_bwd_states(z_all, tabs["kdb"], tabs["cdb"], n_heads)
    b_s_rows = jnp.broadcast_to(sgu_b_s[0][:, :, None], sgu_b_s[0].shape + (HEAD_DIM,))
    w_r, b_r = _router_weights(moe_w_r1[0], moe_b_r1[0], moe_w_r2[0], moe_b_r2[0])
    x_mid, f0, lg0 = _ab_mix(z_all, r_states, x_all, modvec[0], ab_w_out[0].astype(BF16),
                             sgu_w_s[0].astype(BF16), b_s_rows, tabs, g_ffn[0:1], w_r, b_r, n_lat_blocks)
    n0 = b * (t + ctx_len)
    y0 = _hier_moe(f0.reshape(n0, d), lg0.reshape(n0, LANES), moe_w1, moe_w3, moe_w2, 0)
    ctx_out = _gated_residual(x_mid, y0, modvec[0], n_lat_blocks, ctx_len // ROW_BLOCK, 1)
    del ctx_out

    p = _cv_in_proj(x_mid, y0, modvec[0], modvec[1], g_mix[1:2], cv_w_in[0].astype(BF16), cv_conv_w[0],
                    cv_conv_b[0:1], n_lat_blocks)
    w_r, b_r = _router_weights(moe_w_r1[1], moe_b_r1[1], moe_w_r2[1], moe_b_r2[1])
    x2, f1, lg1 = _cv_out_proj(p, x_mid, y0, modvec[0], modvec[1], cv_w_out[0].astype(BF16), g_ffn[1:2],
                               w_r, b_r)
    y1 = _hier_moe(f1.reshape(b * t, d), lg1.reshape(b * t, LANES), moe_w1, moe_w3, moe_w2, 1)
    return _gated_residual(x2, y1, modvec[1], 0, n_lat_blocks, 0, g_final.reshape(1, d))
```

```python
import functools

import jax
import jax.numpy as jnp
import numpy as np
from jax import lax
from jax.experimental import pallas as pl
from jax.experimental.pallas import tpu as pltpu

F32 = jnp.float32
BF16 = jnp.bfloat16

EPS = 1e-6
GRID_W = 64
HEAD_DIM = 128
CHUNK = 128
ROPE_BASE = 10000.0
N_GROUPS = 4
EXPERTS_PER_GROUP = 4
N_PAIRS = 6
N_CLASSES = N_GROUPS * N_PAIRS
PAIR_SLOT_A = (0, 2, 2, 3, 3, 3)
PAIR_SLOT_B = (1, 1, 0, 0, 1, 2)
LANES = 128
ROW_BLOCK = 256
MOE_TILE = 128
COL_CHUNK = 512
STATE_HEADS_PER_STEP = 4
CAST_ROWS = 256
VMEM_LIMIT = 56 * 1024 * 1024
MOE_UP_VMEM_LIMIT = 60 * 1024 * 1024
GELU_C = float(np.float32(np.sqrt(2.0 / np.pi)))


def _cparams(n_axes):
    return pltpu.CompilerParams(dimension_semantics=("arbitrary",) * n_axes,
                                vmem_limit_bytes=VMEM_LIMIT)


def _resident(shape):
    zeros = (0,) * len(shape)
    return pl.BlockSpec(shape, lambda *_: zeros, pipeline_mode=pl.Buffered(1))


def _sigmoid(x):
    return 1.0 / (1.0 + jnp.exp(-x))


def _gelu_tanh(x):
    return x * (0.5 * (1.0 + jnp.tanh(GELU_C * (x + 0.044715 * (x * x * x)))))


def _modulate(x, g, shift, scale):
    ms = jnp.mean(x * x, axis=-1, keepdims=True)
    return (x * lax.rsqrt(ms + EPS) * g) * (1.0 + scale) + shift


def _standardise(y):
    mu = jnp.mean(y, axis=-1, keepdims=True)
    d = y - mu
    var = jnp.mean(d * d, axis=-1, keepdims=True)
    return d * lax.rsqrt(var + EPS)


def _mod_kernel(c_ref, w_ref, b_ref, o_ref):
    c = c_ref[...]
    s = (c * _sigmoid(c)).astype(BF16)
    o_ref[0] = jnp.dot(s, w_ref[0].astype(BF16), preferred_element_type=F32) + b_ref[0]


def _mod_vectors(cvec, w_mod, b_mod):
    depth, d, n = w_mod.shape
    bn = 1536
    return pl.pallas_call(
        _mod_kernel,
        grid=(depth, n // bn),
        in_specs=[pl.BlockSpec((8, d), lambda l, j: (0, 0)),
                  pl.BlockSpec((1, d, bn), lambda l, j: (l, 0, j)),
                  pl.BlockSpec((1, 1, bn), lambda l, j: (l, 0, j))],
        out_specs=pl.BlockSpec((1, 8, bn), lambda l, j: (l, 0, j)),
        out_shape=jax.ShapeDtypeStruct((depth, 8, n), F32),
        compiler_params=_cparams(2),
        name="mod_vectors",
    )(cvec, w_mod, b_mod.reshape(depth, 1, n))


def _ab_in_kernel(x_ref, mod_ref, g_ref, w_ref, cos_ref, sin_ref, z_ref, *, ret_width, qk_scale):
    mod = mod_ref[0, 0]
    a = _modulate(x_ref[0], g_ref[...], mod[0:1], mod[1:2]).astype(BF16)
    cosf = cos_ref[...]
    sinf = sin_ref[...]
    n_out = w_ref.shape[1]
    for j in range(n_out // COL_CHUNK):
        c0 = j * COL_CHUNK
        z = jnp.dot(a, w_ref[:, c0:c0 + COL_CHUNK], preferred_element_type=F32)
        if c0 < 2 * ret_width:
            parts = []
            for hh in range(COL_CHUNK // HEAD_DIM):
                zh = z[:, hh * HEAD_DIM:(hh + 1) * HEAD_DIM]
                zh = zh * cosf + pltpu.roll(zh, HEAD_DIM // 2, axis=1) * sinf
                if c0 < ret_width:
                    zh = zh * qk_scale
                parts.append(zh)
            z = jnp.concatenate(parts, axis=1)
        z_ref[0, :, c0:c0 + COL_CHUNK] = z.astype(BF16)


def _ab_in_proj(x_all, modvec, g_mix, w_in, cosf, sinf, n_lat_blocks):
    b, rows, d = x_all.shape
    n_out = w_in.shape[1]
    ret_width = n_out // 6
    kern = functools.partial(_ab_in_kernel, ret_width=ret_width, qk_scale=HEAD_DIM ** -0.5)
    return pl.pallas_call(
        kern,
        grid=(b, rows // ROW_BLOCK),
        in_specs=[pl.BlockSpec((1, ROW_BLOCK, d), lambda bi, i: (bi, i, 0)),
                  pl.BlockSpec((1, 1, 6, d), lambda bi, i: (bi, i // n_lat_blocks, 0, 0)),
                  _resident((1, d)),
                  _resident((d, n_out)),
                  pl.BlockSpec((ROW_BLOCK, HEAD_DIM), lambda bi, i: (i, 0)),
                  pl.BlockSpec((ROW_BLOCK, HEAD_DIM), lambda bi, i: (i, 0))],
        out_specs=pl.BlockSpec((1, ROW_BLOCK, n_out), lambda bi, i: (bi, i, 0)),
        out_shape=jax.ShapeDtypeStruct((b, rows, n_out), BF16),
        compiler_params=_cparams(2),
        name="ab_in_proj",
    )(x_all, modvec, g_mix, w_in, cosf, sinf)


def _ret_state_kernel(k_ref, v_ref, kdec_ref, cdec_ref, r_ref, s_scr, *, n_chunks):
    s_scr[...] = jnp.zeros_like(s_scr)

    def body(i, carry):
        c = n_chunks - 1 - i
        rows = pl.ds(pl.multiple_of(c * CHUNK, CHUNK), CHUNK)
        for hh in range(s_scr.shape[0]):
            cols = slice(hh * HEAD_DIM, (hh + 1) * HEAD_DIM)
            r_ref[0, c, hh] = s_scr[hh].astype(BF16)
            kd = (k_ref[0, rows, cols].astype(F32) * kdec_ref[hh]).astype(BF16)
            u = lax.dot_general(kd, v_ref[0, rows, cols], (((0,), (0,)), ((), ())),
                                preferred_element_type=F32)
            s_scr[hh] = s_scr[hh] * cdec_ref[hh] + u
        return carry

    lax.fori_loop(0, n_chunks, body, 0)


def _ret_bwd_states(z_all, kdec_b, cdec_b, n_heads):
    b, rows, _ = z_all.shape
    n_chunks = rows // CHUNK
    hps = STATE_HEADS_PER_STEP
    wblk = hps * HEAD_DIM
    kern = functools.partial(_ret_state_kernel, n_chunks=n_chunks)
    return pl.pallas_call(
        kern,
        grid=(b, n_heads // hps),
        in_specs=[pl.BlockSpec((1, rows, wblk), lambda bi, g: (bi, 0, n_heads // hps + g)),
                  pl.BlockSpec((1, rows, wblk), lambda bi, g: (bi, 0, 2 * (n_heads // hps) + g)),
                  pl.BlockSpec((hps, CHUNK, HEAD_DIM), lambda bi, g: (g, 0, 0)),
                  pl.BlockSpec((hps, 1, HEAD_DIM), lambda bi, g: (g, 0, 0))],
        out_specs=pl.BlockSpec((1, n_chunks, hps, HEAD_DIM, HEAD_DIM), lambda bi, g: (bi, 0, g, 0, 0)),
        out_shape=jax.ShapeDtypeStruct((b, n_chunks, n_heads, HEAD_DIM, HEAD_DIM), BF16),
        scratch_shapes=[pltpu.VMEM((hps, HEAD_DIM, HEAD_DIM), F32)],
        compiler_params=_cparams(2),
        name="ret_bwd_states",
    )(z_all, z_all, kdec_b, cdec_b)


def _residual_ffn_in(x, m, mod, gffn, wr_ref, br_ref, xo_ref, f_ref, lg_ref):
    xn = x + mod[2:3] * m
    xo_ref[0] = xn
    f = _modulate(xn, gffn, mod[3:4], mod[4:5])
    f_ref[0] = f
    lg_ref[0] = jnp.dot(f.astype(BF16), wr_ref[...], preferred_element_type=F32) + br_ref[...]


def _ab_mix_kernel(z_ref, r_ref, x_ref, mod_ref, wout_ref, ws_ref, bs_ref, dmat_ref, qdf_ref, qdb_ref,
                   kdf_ref, cdf_ref, gffn_ref, wr_ref, br_ref, xo_ref, f_ref, lg_ref, s_scr, rs_scr,
                   *, n_heads, n_sgu):
    @pl.when(pl.program_id(1) == 0)
    def _():
        s_scr[...] = jnp.zeros_like(s_scr)

    ret_w = n_heads * HEAD_DIM
    sgu_w = n_sgu * HEAD_DIM
    nt = (((1,), (1,)), ((), ()))
    tn = (((0,), (0,)), ((), ()))
    for c in range(ROW_BLOCK // CHUNK):
        r0 = c * CHUNK
        for h in range(n_heads):
            c0 = h * HEAD_DIM
            q = z_ref[0, r0:r0 + CHUNK, c0:c0 + HEAD_DIM]
            k = z_ref[0, r0:r0 + CHUNK, ret_w + c0:ret_w + c0 + HEAD_DIM]
            v = z_ref[0, r0:r0 + CHUNK, 2 * ret_w + c0:2 * ret_w + c0 + HEAD_DIM]
            g = z_ref[0, r0:r0 + CHUNK, 3 * ret_w + c0:3 * ret_w + c0 + HEAD_DIM].astype(F32)
            sc = lax.dot_general(q, k, nt, preferred_element_type=F32) * dmat_ref[h]
            y = jnp.dot(sc.astype(BF16), v, preferred_element_type=F32)
            y = y + jnp.dot(q, s_scr[h].astype(BF16), preferred_element_type=F32) * qdf_ref[h]
            y = y + jnp.dot(q, r_ref[0, c, h], preferred_element_type=F32) * qdb_ref[h]
            r = (g * _sigmoid(g)) * _standardise(y)
            rs_scr[r0:r0 + CHUNK, c0:c0 + HEAD_DIM] = r.astype(BF16)
            kd = (k.astype(F32) * kdf_ref[h]).astype(BF16)
            s_scr[h] = s_scr[h] * cdf_ref[h] + lax.dot_general(kd, v, tn, preferred_element_type=F32)
        for gi in range(n_sgu):
            c0 = 4 * ret_w + gi * HEAD_DIM
            u = _gelu_tanh(z_ref[0, r0:r0 + CHUNK, c0:c0 + HEAD_DIM].astype(F32))
            vv = _gelu_tanh(z_ref[0, r0:r0 + CHUNK, sgu_w + c0:sgu_w + c0 + HEAD_DIM].astype(F32))
            vn = _standardise(vv).astype(BF16)
            sg = jnp.dot(ws_ref[gi], vn, preferred_element_type=F32) + bs_ref[gi]
            rs_scr[r0:r0 + CHUNK, ret_w + gi * HEAD_DIM:ret_w + (gi + 1) * HEAD_DIM] = (u * sg).astype(BF16)
    m = jnp.dot(rs_scr[...], wout_ref[...], preferred_element_type=F32)
    _residual_ffn_in(x_ref[0], m, mod_ref[0, 0], gffn_ref[...], wr_ref, br_ref, xo_ref, f_ref, lg_ref)


def _ab_mix(z_all, r_states, x_all, modvec, w_out, w_s, b_s_rows, tabs, g_ffn, w_r, b_r, n_lat_blocks):
    b, rows, d = x_all.shape
    n_out = z_all.shape[2]
    n_heads = r_states.shape[2]
    n_sgu = w_s.shape[0]
    nblk = rows // ROW_BLOCK
    cpb = ROW_BLOCK // CHUNK
    blk = lambda i: (i + n_lat_blocks) % nblk
    kern = functools.partial(_ab_mix_kernel, n_heads=n_heads, n_sgu=n_sgu)
    tab_spec = _resident((n_heads, CHUNK, HEAD_DIM))
    row_out = lambda w, dt: (pl.BlockSpec((1, ROW_BLOCK, w), lambda bi, i: (bi, blk(i), 0)),
                             jax.ShapeDtypeStruct((b, rows, w), dt))
    outs = [row_out(d, F32), row_out(d, F32), row_out(LANES, F32)]
    return pl.pallas_call(
        kern,
        grid=(b, nblk),
        in_specs=[pl.BlockSpec((1, ROW_BLOCK, n_out), lambda bi, i: (bi, blk(i), 0)),
                  pl.BlockSpec((1, cpb, n_heads, HEAD_DIM, HEAD_DIM), lambda bi, i: (bi, blk(i), 0, 0, 0)),
                  pl.BlockSpec((1, ROW_BLOCK, d), lambda bi, i: (bi, blk(i), 0)),
                  pl.BlockSpec((1, 1, 6, d), lambda bi, i: (bi, blk(i) // n_lat_blocks, 0, 0)),
                  _resident(w_out.shape),
                  _resident(w_s.shape),
                  _resident(b_s_rows.shape),
                  tab_spec, tab_spec, tab_spec, tab_spec,
                  _resident((n_heads, 1, HEAD_DIM)),
                  _resident((1, d)),
                  _resident(w_r.shape),
                  _resident((1, LANES))],
        out_specs=[o[0] for o in outs],
        out_shape=[o[1] for o in outs],
        scratch_shapes=[pltpu.VMEM((n_heads, HEAD_DIM, HEAD_DIM), F32),
                        pltpu.VMEM((ROW_BLOCK, d), BF16)],
        compiler_params=_cparams(2),
        name="ab_mix",
    )(z_all, r_states, x_all, modvec, w_out, w_s, b_s_rows, tabs["dmat"], tabs["qdf"], tabs["qdb"],
      tabs["kdf"], tabs["cdf"], g_ffn, w_r, b_r)


def _cv_in_kernel(x_ref, y_ref, mod0_ref, mod_ref, g_ref, w_ref, cw_ref, cb_ref, p_ref):
    mod = mod_ref[0, 0]
    x1 = x_ref[0] + mod0_ref[0, 0][5:6] * y_ref[...]
    a = _modulate(x1, g_ref[...], mod[0:1], mod[1:2]).astype(BF16)
    bm = a.shape[0]
    width = p_ref.shape[2]
    col = lax.broadcasted_iota(jnp.int32, (bm, COL_CHUNK), 0) % GRID_W
    for j in range(width // COL_CHUNK):
        c0 = j * COL_CHUNK
        gate_b = jnp.dot(a, w_ref[:, c0:c0 + COL_CHUNK], preferred_element_type=F32)
        gate_c = jnp.dot(a, w_ref[:, width + c0:width + c0 + COL_CHUNK], preferred_element_type=F32)
        hv = jnp.dot(a, w_ref[:, 2 * width + c0:2 * width + c0 + COL_CHUNK], preferred_element_type=F32)
        xg = gate_c * hv
        left = jnp.where(col == 0, 0.0, pltpu.roll(xg, 1, axis=0))
        right = jnp.where(col == GRID_W - 1, 0.0, pltpu.roll(xg, bm - 1, axis=0))
        y = cb_ref[:, c0:c0 + COL_CHUNK] + left * cw_ref[0:1, c0:c0 + COL_CHUNK]
        y = y + xg * cw_ref[1:2, c0:c0 + COL_CHUNK]
        y = y + right * cw_ref[2:3, c0:c0 + COL_CHUNK]
        p_ref[0, :, c0:c0 + COL_CHUNK] = (gate_b * y).astype(BF16)


def _cv_in_proj(x_mid, y_rows, modvec_prev, modvec, g_mix, w_in, conv_w, conv_b, n_lat_blocks):
    b, rows, d = x_mid.shape
    nblk = rows // ROW_BLOCK
    width = conv_w.shape[1]
    t = n_lat_blocks * ROW_BLOCK
    lat_mod = pl.BlockSpec((1, 1, 6, d), lambda bi, i: (bi, 0, 0, 0))
    return pl.pallas_call(
        _cv_in_kernel,
        grid=(b, n_lat_blocks),
        in_specs=[pl.BlockSpec((1, ROW_BLOCK, d), lambda bi, i: (bi, i, 0)),
                  pl.BlockSpec((ROW_BLOCK, d), lambda bi, i: (bi * nblk + i, 0)),
                  lat_mod, lat_mod,
                  _resident((1, d)),
                  _resident(w_in.shape),
                  _resident(conv_w.shape),
                  _resident((1, width))],
        out_specs=pl.BlockSpec((1, ROW_BLOCK, width), lambda bi, i: (bi, i, 0)),
        out_shape=jax.ShapeDtypeStruct((b, t, width), BF16),
        compiler_params=_cparams(2),
        name="cv_in_proj",
    )(x_mid, y_rows, modvec_prev, modvec, g_mix, w_in, conv_w, conv_b)


def _cv_out_kernel(p_ref, x_ref, y_ref, mod0_ref, mod_ref, wout_ref, gffn_ref, wr_ref, br_ref,
                   xo_ref, f_ref, lg_ref):
    x1 = x_ref[0] + mod0_ref[0, 0][5:6] * y_ref[...]
    m = jnp.dot(p_ref[0], wout_ref[...], preferred_element_type=F32)
    _residual_ffn_in(x1, m, mod_ref[0, 0], gffn_ref[...], wr_ref, br_ref, xo_ref, f_ref, lg_ref)


def _cv_out_proj(p, x_mid, y_rows, modvec_prev, modvec, w_out, g_ffn, w_r, b_r):
    b, t, width = p.shape
    d = w_out.shape[1]
    nblk = x_mid.shape[1] // ROW_BLOCK
    row_out = lambda w: (pl.BlockSpec((1, ROW_BLOCK, w), lambda bi, i: (bi, i, 0)),
                         jax.ShapeDtypeStruct((b, t, w), F32))
    outs = [row_out(d), row_out(d), row_out(LANES)]
    lat_mod = pl.BlockSpec((1, 1, 6, d), lambda bi, i: (bi, 0, 0, 0))
    return pl.pallas_call(
        _cv_out_kernel,
        grid=(b, t // ROW_BLOCK),
        in_specs=[pl.BlockSpec((1, ROW_BLOCK, width), lambda bi, i: (bi, i, 0)),
                  pl.BlockSpec((1, ROW_BLOCK, d), lambda bi, i: (bi, i, 0)),
                  pl.BlockSpec((ROW_BLOCK, d), lambda bi, i: (bi * nblk + i, 0)),
                  lat_mod, lat_mod,
                  _resident(w_out.shape),
                  _resident((1, d)),
                  _resident(w_r.shape),
                  _resident((1, LANES))],
        out_specs=[o[0] for o in outs],
        out_shape=[o[1] for o in outs],
        compiler_params=_cparams(2),
        name="cv_out_proj",
    )(p, x_mid, y_rows, modvec_prev, modvec, w_out, g_ffn, w_r, b_r)


def _route_kernel(lg_ref, info_ref, cnt_ref, cnt_scr):
    @pl.when(pl.program_id(0) == 0)
    def _():
        cnt_scr[...] = jnp.zeros_like(cnt_scr)

    lg = lg_ref[...]
    bm = lg.shape[0]
    lane = lax.broadcasted_iota(jnp.int32, lg.shape, 1).astype(F32)
    neg = -jnp.inf

    def first_max(vals):
        mx = jnp.max(vals, axis=1, keepdims=True)
        idx = jnp.min(jnp.where(vals == mx, lane, float(LANES)), axis=1, keepdims=True)
        return mx, idx

    is_grp = lane < N_GROUPS
    m1, grp = first_max(jnp.where(is_grp, lg, neg))
    den = jnp.sum(jnp.where(is_grp, jnp.exp(lg - m1), 0.0), axis=1, keepdims=True)
    p_grp = 1.0 / den
    base = N_GROUPS + EXPERTS_PER_GROUP * grp
    in_grp = (lane >= base) & (lane < base + EXPERTS_PER_GROUP)
    l2 = jnp.where(in_grp, lg, neg)
    v1, i1 = first_max(l2)
    v2, i2 = first_max(jnp.where(lane == i1, neg, l2))
    e = jnp.exp(v2 - v1)
    w_top1 = p_grp * (1.0 / (1.0 + e))
    w_top2 = p_grp * (e / (1.0 + e))
    first_lo = i1 < i2
    lo = jnp.where(first_lo, i1, i2) - base
    hi = jnp.where(first_lo, i2, i1) - base
    w_lo = jnp.where(first_lo, w_top1, w_top2)
    w_hi = jnp.where(first_lo, w_top2, w_top1)
    pair = jnp.where(lo == 0.0, jnp.where(hi == 1.0, 0.0, hi),
                     jnp.where(lo == 1.0, jnp.where(hi == 2.0, 1.0, 4.0), 5.0))
    w_a = jnp.where(pair == 0.0, w_lo, w_hi)
    w_b = jnp.where(pair == 0.0, w_hi, w_lo)
    cls = grp * N_PAIRS + pair

    onehot = (lane == cls).astype(F32)
    tri = (lax.broadcasted_iota(jnp.int32, (bm, bm), 0) > lax.broadcasted_iota(jnp.int32, (bm, bm), 1))
    before = jnp.dot(tri.astype(BF16), onehot.astype(BF16), preferred_element_type=F32)
    rank = jnp.sum(onehot * (before + cnt_scr[...]), axis=1, keepdims=True)
    cnt_scr[...] = cnt_scr[...] + jnp.sum(onehot, axis=0, keepdims=True)
    cnt_ref[...] = cnt_scr[...]
    info = jnp.where(lane == 0, cls,
                     jnp.where(lane == 1, rank,
                               jnp.where(lane == 2, w_a, jnp.where(lane == 3, w_b, 0.0))))
    info_ref[...] = info


def _route(logits):
    n = logits.shape[0]
    return pl.pallas_call(
        _route_kernel,
        grid=(n // ROW_BLOCK,),
        in_specs=[pl.BlockSpec((ROW_BLOCK, LANES), lambda i: (i, 0))],
        out_specs=[pl.BlockSpec((ROW_BLOCK, LANES), lambda i: (i, 0)),
                   pl.BlockSpec((1, LANES), lambda i: (0, 0))],
        out_shape=[jax.ShapeDtypeStruct((n, LANES), F32), jax.ShapeDtypeStruct((1, LANES), F32)],
        scratch_shapes=[pltpu.VMEM((1, LANES), F32)],
        compiler_params=_cparams(1),
        name="moe_route",
    )(logits)


def _dispatch_plan(info, counts, n_tok, expert_base):
    tm = MOE_TILE
    n_tiles = n_tok // tm + N_CLASSES
    n_rows = n_tiles * tm
    i32 = jnp.int32
    slot_a = jnp.asarray(PAIR_SLOT_A, i32)
    slot_b = jnp.asarray(PAIR_SLOT_B, i32)
    cls = info[:, 0].astype(i32)
    rank = info[:, 1].astype(i32)
    cnt = counts[0, :N_CLASSES].astype(i32)
    tiles_c = (cnt + tm - 1) // tm
    tile_end = jnp.cumsum(tiles_c)
    tile_start = tile_end - tiles_c
    pos = tile_start[cls] * tm + rank
    tok = jnp.arange(n_tok, dtype=F32)[:, None]
    per_row = jnp.zeros((n_rows, 3), F32).at[pos].set(jnp.concatenate([tok, info[:, 2:4]], axis=1))
    src = per_row[:, 0].astype(i32).reshape(n_tiles, tm)
    w_rows = per_row[:, 1:3]
    n_used = tile_end[-1]

    def class_of(step, ends):
        return jnp.minimum(jnp.sum((ends[None, :] <= step[:, None]).astype(i32), axis=1), N_CLASSES - 1)

    tile = jnp.arange(n_tiles, dtype=i32)
    tcls = class_of(jnp.minimum(tile, n_used - 1), tile_end)
    e_a = expert_base + (tcls // N_PAIRS) * EXPERTS_PER_GROUP + slot_a[tcls % N_PAIRS]
    e_b = expert_base + (tcls // N_PAIRS) * EXPERTS_PER_GROUP + slot_b[tcls % N_PAIRS]
    n_valid = jnp.clip(cnt[tcls] - (tile - tile_start[tcls]) * tm, 0, tm)
    n_valid = jnp.where(tile < n_used, n_valid, 0).astype(i32)
    row = jnp.arange(tm, dtype=i32)[None, :]
    dump = n_tok + (tile % 2)[:, None] * tm + row
    dst = jnp.where(row < n_valid[:, None], src, dump)

    return dict(src=src.reshape(n_tiles, 1, tm), dst=dst.reshape(n_tiles, 1, tm), w_rows=w_rows,
                n_valid=n_valid, n_tiles=n_tiles, a_runs=_weight_runs(e_a, n_used),
                b_runs=_weight_runs(e_b, n_used), e_a=e_a.astype(i32), e_b=e_b.astype(i32))


def _weight_runs(expert, n_real):
    i32 = jnp.int32
    n = expert.shape[0]
    idx = jnp.arange(n, dtype=i32)
    prev = jnp.concatenate([jnp.full((1,), -1, i32), expert[:-1].astype(i32)])
    first = ((expert != prev) & (idx < n_real)).astype(i32)
    parity = (jnp.cumsum(first) - 1) % 2
    later = (idx[None, :] > idx[:, None]) & (first[None, :] == 1)
    nxt_idx = jnp.min(jnp.where(later, idx[None, :], n), axis=1)
    nxt = jnp.where(nxt_idx < n, expert[jnp.minimum(nxt_idx, n - 1)], -1)
    return first, parity.astype(i32), nxt.astype(i32)


def _row_copy(src_hbm, src_row, dst_buf, dst_row, sem):
    return pltpu.make_async_copy(src_hbm.at[pl.ds(src_row, 1), :], dst_buf.at[pl.ds(dst_row, 1), :], sem)


def _cast_rows(src, dst):
    def body(c, carry):
        rows = pl.ds(pl.multiple_of(c * CAST_ROWS, CAST_ROWS), CAST_ROWS)
        dst[rows, :] = src[rows, :].astype(BF16)
        return carry
    lax.fori_loop(0, src.shape[0] // CAST_ROWS, body, 0)


def _moe_up_kernel(nv_ref, ea_ref, fa_ref, na_ref, eb_ref, fb_ref, nb_ref,
                   src_ref, nsrc_ref, f_hbm, ws_ref, w1_hbm, w3_hbm, h_ref,
                   xbuf, xsem, wf32, wbf, wsem):
    i = pl.program_id(0)
    tm = xbuf.shape[1]
    xs = i % 2
    half = wbf.shape[3]
    runs = ((ea_ref, fa_ref, na_ref), (eb_ref, fb_ref, nb_ref))
    used = lambda tile: nv_ref[jnp.minimum(tile, pl.num_programs(0) - 1)] > 0

    def fetch(s, expert):
        pltpu.make_async_copy(w1_hbm.at[expert], wf32.at[s, 0], wsem.at[s]).start()
        pltpu.make_async_copy(w3_hbm.at[expert], wf32.at[s, 1], wsem.at[s]).start()

    def gather(idx_ref, s):
        for r in range(tm):
            _row_copy(f_hbm, idx_ref[0, 0, r], xbuf.at[s], r, xsem.at[s]).start()

    @pl.when(i == 0)
    def _():
        for s, (e_ref, _, _) in enumerate(runs):
            fetch(s, e_ref[0])
        gather(src_ref, 0)

    @pl.when(used(i))
    def _():
        for s, (_, f_ref, n_ref) in enumerate(runs):
            @pl.when(f_ref[i] == 1)
            def _():
                for m in range(2):
                    pltpu.make_async_copy(w1_hbm.at[0], wf32.at[s, m], wsem.at[s]).wait()
                for m in range(2):
                    _cast_rows(wf32.at[s, m], wbf.at[s, m])

                @pl.when(n_ref[i] >= 0)
                def _():
                    fetch(s, n_ref[i])

        @pl.when((i + 1 < pl.num_programs(0)) & used(i + 1))
        def _():
            gather(nsrc_ref, 1 - xs)

        pltpu.make_async_copy(f_hbm.at[pl.ds(0, tm), :], xbuf.at[xs], xsem.at[xs]).wait()
        x = xbuf[xs].astype(BF16)
        ws = ws_ref[...]
        for s in range(2):
            up = jnp.dot(x, wbf[s, 0], preferred_element_type=F32)
            gate = jnp.dot(x, wbf[s, 1], preferred_element_type=F32)
            act = (up * _sigmoid(up)) * gate
            h_ref[:, s * half:(s + 1) * half] = (act * ws[:, s:s + 1]).astype(BF16)

    @pl.when(jnp.logical_not(used(i)))
    def _():
        h_ref[...] = jnp.zeros_like(h_ref)


def _moe_up(f_rows, plan, w1, w3):
    n_tiles = plan["n_tiles"]
    tm = MOE_TILE
    d = f_rows.shape[1]
    n_e, _, de = w1.shape
    last = n_tiles - 1
    (fa, _, na), (fb, _, nb) = plan["a_runs"], plan["b_runs"]
    grid_spec = pltpu.PrefetchScalarGridSpec(
        num_scalar_prefetch=7,
        grid=(n_tiles,),
        in_specs=[pl.BlockSpec((1, 1, tm), lambda i, *_: (i, 0, 0), memory_space=pltpu.SMEM),
                  pl.BlockSpec((1, 1, tm), lambda i, *_: (jnp.minimum(i + 1, last), 0, 0),
                               memory_space=pltpu.SMEM),
                  pl.BlockSpec(memory_space=pl.ANY),
                  pl.BlockSpec((tm, 2), lambda i, *_: (i, 0)),
                  pl.BlockSpec(memory_space=pl.ANY),
                  pl.BlockSpec(memory_space=pl.ANY)],
        out_specs=pl.BlockSpec((tm, 2 * de), lambda i, *_: (i, 0)),
        scratch_shapes=[pltpu.VMEM((2, tm, d), F32), pltpu.SemaphoreType.DMA((2,)),
                        pltpu.VMEM((2, 2, d, de), F32), pltpu.VMEM((2, 2, d, de), BF16),
                        pltpu.SemaphoreType.DMA((2,))],
    )
    return pl.pallas_call(
        _moe_up_kernel,
        grid_spec=grid_spec,
        out_shape=jax.ShapeDtypeStruct((n_tiles * tm, 2 * de), BF16),
        compiler_params=pltpu.CompilerParams(dimension_semantics=("arbitrary",),
                                             vmem_limit_bytes=MOE_UP_VMEM_LIMIT),
        name="moe_up",
    )(plan["n_valid"], plan["e_a"], fa, na, plan["e_b"], fb, nb,
      plan["src"], plan["src"], f_rows, plan["w_rows"], w1, w3)


def _moe_down_kernel(nv_ref, ea_ref, fa_ref, pa_ref, na_ref, eb_ref, fb_ref, pb_ref, nb_ref,
                     dst_ref, h_ref, w2_hbm, y_hbm, obuf, sems, wf32, wbf, wsem):
    i = pl.program_id(0)
    n_steps = pl.num_programs(0)
    slot = i % 2
    half = wbf.shape[1]
    tm = obuf.shape[1]
    runs = ((ea_ref, fa_ref, pa_ref, na_ref), (eb_ref, fb_ref, pb_ref, nb_ref))

    def fetch(s, expert, buf):
        pltpu.make_async_copy(w2_hbm.at[expert], wf32.at[s, buf], wsem.at[s, buf]).start()

    def drain(tile, s):
        @pl.when(nv_ref[tile] > 0)
        def _():
            pltpu.make_async_copy(obuf.at[s], y_hbm.at[pl.ds(0, tm), :], sems.at[s]).wait()

    @pl.when(i == 0)
    def _():
        n_tok = y_hbm.shape[0] - 2 * tm
        obuf[...] = jnp.zeros_like(obuf)
        for s in range(2):
            fill = pltpu.make_async_copy(obuf.at[s], y_hbm.at[pl.ds(n_tok + s * tm, tm), :], sems.at[s])
            fill.start()
            fill.wait()
        for s, (e_ref, _, _, _) in enumerate(runs):
            fetch(s, e_ref[0], 0)

    @pl.when(i >= 2)
    def _():
        drain(i - 2, slot)

    @pl.when(nv_ref[i] > 0)
    def _():
        for s, (_, f_ref, p_ref, n_ref) in enumerate(runs):
            @pl.when(f_ref[i] == 1)
            def _():
                buf = p_ref[i]
                pltpu.make_async_copy(w2_hbm.at[0], wf32.at[s, buf], wsem.at[s, buf]).wait()

                @pl.when(n_ref[i] >= 0)
                def _():
                    fetch(s, n_ref[i], 1 - buf)
                _cast_rows(wf32.at[s, buf], wbf.at[s])

        out = jnp.dot(h_ref[:, :half], wbf[0], preferred_element_type=F32)
        out = out + jnp.dot(h_ref[:, half:], wbf[1], preferred_element_type=F32)
        obuf[slot] = out
        for r in range(tm):
            _row_copy(obuf.at[slot], r, y_hbm, dst_ref[0, 0, r], sems.at[slot]).start()

    @pl.when(i == n_steps - 1)
    def _():
        @pl.when(i >= 1)
        def _():
            drain(i - 1, 1 - slot)
        drain(i, slot)


def _moe_down(h_rows, plan, w2, n_tok):
    n_tiles = plan["n_tiles"]
    tm = MOE_TILE
    n_e, de, d = w2.shape
    grid_spec = pltpu.PrefetchScalarGridSpec(
        num_scalar_prefetch=9,
        grid=(n_tiles,),
        in_specs=[pl.BlockSpec((1, 1, tm), lambda i, *_: (i, 0, 0), memory_space=pltpu.SMEM),
                  pl.BlockSpec((tm, 2 * de), lambda i, *_: (i, 0)),
                  pl.BlockSpec(memory_space=pl.ANY)],
        out_specs=pl.BlockSpec(memory_space=pl.ANY),
        scratch_shapes=[pltpu.VMEM((2, tm, d), F32), pltpu.SemaphoreType.DMA((2,)),
                        pltpu.VMEM((2, 2, de, d), F32), pltpu.VMEM((2, de, d), BF16),
                        pltpu.SemaphoreType.DMA((2, 2))],
    )
    return pl.pallas_call(
        _moe_down_kernel,
        grid_spec=grid_spec,
        out_shape=jax.ShapeDtypeStruct((n_tok + 2 * tm, d), F32),
        compiler_params=_cparams(1),
        name="moe_down",
    )(plan["n_valid"], plan["e_a"], *plan["a_runs"], plan["e_b"], *plan["b_runs"], plan["dst"], h_rows, w2)


def _hier_moe(f_rows, logits, w1, w3, w2, layer):
    n_tok = f_rows.shape[0]
    n_layers, n_exp = w1.shape[:2]
    merged = lambda w: w.reshape((n_layers * n_exp,) + w.shape[2:])
    info, counts = _route(logits)
    plan = _dispatch_plan(info, counts, n_tok, layer * n_exp)
    h_rows = _moe_up(f_rows, plan, merged(w1), merged(w3))
    return _moe_down(h_rows, plan, merged(w2), n_tok)


def _resid_kernel(x_ref, y_ref, mod_ref, o_ref):
    o_ref[0] = x_ref[0] + mod_ref[0, 0][5:6] * y_ref[...]


def _final_kernel(x_ref, y_ref, mod_ref, g_ref, o_ref):
    x = x_ref[0] + mod_ref[0, 0][5:6] * y_ref[...]
    ms = jnp.mean(x * x, axis=-1, keepdims=True)
    o_ref[0] = (x * lax.rsqrt(ms + EPS)) * g_ref[...]


def _gated_residual(x, y_rows, modvec, first_block, n_blocks, mod_row, g_final=None):
    b, rows, d = x.shape
    nblk = rows // ROW_BLOCK
    in_specs = [pl.BlockSpec((1, ROW_BLOCK, d), lambda bi, i: (bi, first_block + i, 0)),
                pl.BlockSpec((ROW_BLOCK, d), lambda bi, i: (bi * nblk + first_block + i, 0)),
                pl.BlockSpec((1, 1, 6, d), lambda bi, i: (bi, mod_row, 0, 0))]
    args = [x, y_rows, modvec]
    kern = _resid_kernel
    if g_final is not None:
        in_specs.append(_resident((1, d)))
        args.append(g_final)
        kern = _final_kernel
    return pl.pallas_call(
        kern,
        grid=(b, n_blocks),
        in_specs=in_specs,
        out_specs=pl.BlockSpec((1, ROW_BLOCK, d), lambda bi, i: (bi, i, 0)),
        out_shape=jax.ShapeDtypeStruct((b, n_blocks * ROW_BLOCK, d), F32),
        compiler_params=_cparams(2),
        name="gated_residual" if g_final is None else "final_norm",
    )(*args)


def _rope_tables(t, ctx_len):
    pos = jnp.arange(t)
    row = (pos // GRID_W).astype(F32)
    col = (pos % GRID_W).astype(F32)
    n_freq = HEAD_DIM // 4
    inv = ROPE_BASE ** (-jnp.arange(n_freq, dtype=F32) / n_freq)
    ang = jnp.concatenate([row[:, None] * inv, col[:, None] * inv], axis=-1)
    cos, sin = jnp.cos(ang), jnp.sin(ang)
    cosf = jnp.concatenate([cos, cos], axis=-1)
    sinf = jnp.concatenate([-sin, sin], axis=-1)
    cosf = jnp.concatenate([cosf, jnp.ones((ctx_len, HEAD_DIM), F32)], axis=0)
    sinf = jnp.concatenate([sinf, jnp.zeros((ctx_len, HEAD_DIM), F32)], axis=0)
    return cosf, sinf


def _retention_tables(decay_logit):
    lg_f = jax.nn.log_sigmoid(decay_logit[0].astype(F32))
    lg_b = jax.nn.log_sigmoid(decay_logit[1].astype(F32))
    n_heads = lg_f.shape[0]
    i = jnp.arange(CHUNK, dtype=F32)
    diff = i[:, None] - i[None, :]
    fwd = jnp.exp(jnp.where(diff >= 0, diff[None] * lg_f[:, None, None], -jnp.inf))
    bwd = jnp.exp(jnp.where(diff < 0, -diff[None] * lg_b[:, None, None], -jnp.inf))
    rows = lambda v: jnp.broadcast_to(v[:, :, None], (n_heads, CHUNK, HEAD_DIM))
    return dict(
        dmat=fwd + bwd,
        qdf=rows(jnp.exp((i + 1.0)[None, :] * lg_f[:, None])),
        qdb=rows(jnp.exp((CHUNK - i)[None, :] * lg_b[:, None])),
        kdf=rows(jnp.exp((CHUNK - 1.0 - i)[None, :] * lg_f[:, None])),
        kdb=rows(jnp.exp(i[None, :] * lg_b[:, None])),
        cdf=jnp.broadcast_to(jnp.exp(CHUNK * lg_f)[:, None, None], (n_heads, 1, HEAD_DIM)),
        cdb=jnp.broadcast_to(jnp.exp(CHUNK * lg_b)[:, None, None], (n_heads, 1, HEAD_DIM)),
    )


def _router_weights(w_r1, b_r1, w_r2, b_r2):
    d = w_r1.shape[0]
    n_exp = N_GROUPS * EXPERTS_PER_GROUP
    w = jnp.concatenate([w_r1, w_r2.transpose(1, 0, 2).reshape(d, n_exp)], axis=1)
    b = jnp.concatenate([b_r1, b_r2.reshape(n_exp)])
    pad = LANES - w.shape[1]
    return jnp.pad(w, ((0, 0), (0, pad))).astype(BF16), jnp.pad(b, (0, pad)).reshape(1, LANES)


def kernel(x, c, ctx, c_ctx, w_mod, b_mod, g_mix, g_ffn, ab_w_in, ab_w_out, ret_decay_logit, sgu_w_s, sgu_b_s,
           cv_w_in, cv_conv_w, cv_conv_b, cv_w_out, moe_w_r1, moe_b_r1, moe_w_r2, moe_b_r2, moe_w1, moe_w3,
           moe_w2, g_final):
    b, t, d = x.shape
    ctx_len = ctx.shape[1]
    n_lat_blocks = t // ROW_BLOCK
    n_heads = ret_decay_logit.shape[2]

    cvec = jnp.zeros((8, d), F32).at[:b].set(c).at[b].set(c_ctx)
    mod = _mod_vectors(cvec, w_mod, b_mod)
    mod_lat = mod[:, :b].reshape(2, b, 1, 6, d)
    mod_ctx = jnp.broadcast_to(mod[:, b].reshape(2, 1, 1, 6, d), (2, b, 1, 6, d))
    modvec = jnp.concatenate([mod_lat, mod_ctx], axis=2)

    x_all = jnp.concatenate([x, ctx], axis=1)
    cosf, sinf = _rope_tables(t, ctx_len)
    tabs = _retention_tables(ret_decay_logit[0])
    z_all = _ab_in_proj(x_all, modvec[0], g_mix[0:1], ab_w_in[0].astype(BF16), cosf, sinf, n_lat_blocks)
    r_states = _ret_bwd_states(z_all, tabs["kdb"], tabs["cdb"], n_heads)
    b_s_rows = jnp.broadcast_to(sgu_b_s[0][:, :, None], sgu_b_s[0].shape + (HEAD_DIM,))
    w_r, b_r = _router_weights(moe_w_r1[0], moe_b_r1[0], moe_w_r2[0], moe_b_r2[0])
    x_mid, f0, lg0 = _ab_mix(z_all, r_states, x_all, modvec[0], ab_w_out[0].astype(BF16),
                             sgu_w_s[0].astype(BF16), b_s_rows, tabs, g_ffn[0:1], w_r, b_r, n_lat_blocks)
    n0 = b * (t + ctx_len)
    y0 = _hier_moe(f0.reshape(n0, d), lg0.reshape(n0, LANES), moe_w1, moe_w3, moe_w2, 0)
    ctx_out = _gated_residual(x_mid, y0, modvec[0], n_lat_blocks, ctx_len // ROW_BLOCK, 1)
    del ctx_out

    p = _cv_in_proj(x_mid, y0, modvec[0], modvec[1], g_mix[1:2], cv_w_in[0].astype(BF16), cv_conv_w[0],
                    cv_conv_b[0:1], n_lat_blocks)
    w_r, b_r = _router_weights(moe_w_r1[1], moe_b_r1[1], moe_w_r2[1], moe_b_r2[1])
    x2, f1, lg1 = _cv_out_proj(p, x_mid, y0, modvec[0], modvec[1], cv_w_out[0].astype(BF16), g_ffn[1:2],
                               w_r, b_r)
    y1 = _hier_moe(f1.reshape(b * t, d), lg1.reshape(b * t, LANES), moe_w1, moe_w3, moe_w2, 1)
    return _gated_residual(x2, y1, modvec[1], 0, n_lat_blocks, 0, g_final.reshape(1, d))
```

```python
import functools

import jax
import jax.numpy as jnp
import numpy as np
from jax import lax
from jax.experimental import pallas as pl
from jax.experimental.pallas import tpu as pltpu

F32 = jnp.float32
BF16 = jnp.bfloat16

EPS = 1e-6
GRID_W = 64
HEAD_DIM = 128
CHUNK = 128
ROPE_BASE = 10000.0
N_GROUPS = 4
EXPERTS_PER_GROUP = 4
N_PAIRS = 6
N_CLASSES = N_GROUPS * N_PAIRS
PAIR_SLOT_A = (0, 2, 2, 3, 3, 3)
PAIR_SLOT_B = (1, 1, 0, 0, 1, 2)
LANES = 128
ROW_BLOCK = 256
MOE_TILE = 128
COL_CHUNK = 512
STATE_HEADS_PER_STEP = 4
CAST_ROWS = 256
VMEM_LIMIT = 56 * 1024 * 1024
MOE_UP_VMEM_LIMIT = 60 * 1024 * 1024
GELU_C = float(np.float32(np.sqrt(2.0 / np.pi)))


def _cparams(n_axes):
    return pltpu.CompilerParams(dimension_semantics=("arbitrary",) * n_axes,
                                vmem_limit_bytes=VMEM_LIMIT)


def _resident(shape):
    zeros = (0,) * len(shape)
    return pl.BlockSpec(shape, lambda *_: zeros, pipeline_mode=pl.Buffered(1))


def _sigmoid(x):
    return 1.0 / (1.0 + jnp.exp(-x))


def _gelu_tanh(x):
    return x * (0.5 * (1.0 + jnp.tanh(GELU_C * (x + 0.044715 * (x * x * x)))))


def _modulate(x, g, shift, scale):
    ms = jnp.mean(x * x, axis=-1, keepdims=True)
    return (x * lax.rsqrt(ms + EPS) * g) * (1.0 + scale) + shift


def _standardise(y):
    mu = jnp.mean(y, axis=-1, keepdims=True)
    d = y - mu
    var = jnp.mean(d * d, axis=-1, keepdims=True)
    return d * lax.rsqrt(var + EPS)


def _mod_kernel(c_ref, w_ref, b_ref, o_ref):
    c = c_ref[...]
    s = (c * _sigmoid(c)).astype(BF16)
    o_ref[0] = jnp.dot(s, w_ref[0].astype(BF16), preferred_element_type=F32) + b_ref[0]


def _mod_vectors(cvec, w_mod, b_mod):
    depth, d, n = w_mod.shape
    bn = 1536
    return pl.pallas_call(
        _mod_kernel,
        grid=(depth, n // bn),
        in_specs=[pl.BlockSpec((8, d), lambda l, j: (0, 0)),
                  pl.BlockSpec((1, d, bn), lambda l, j: (l, 0, j)),
                  pl.BlockSpec((1, 1, bn), lambda l, j: (l, 0, j))],
        out_specs=pl.BlockSpec((1, 8, bn), lambda l, j: (l, 0, j)),
        out_shape=jax.ShapeDtypeStruct((depth, 8, n), F32),
        compiler_params=_cparams(2),
        name="mod_vectors",
    )(cvec, w_mod, b_mod.reshape(depth, 1, n))


def _ab_in_kernel(x_ref, c_ref, mod_ref, g_ref, w_ref, cos_ref, sin_ref, z_ref, *, ret_width, qk_scale,
                  n_lat_blocks):
    mod = mod_ref[0, 0]
    rows_in = jnp.where(pl.program_id(1) < n_lat_blocks, x_ref[0], c_ref[0])
    a = _modulate(rows_in, g_ref[...], mod[0:1], mod[1:2]).astype(BF16)
    cosf = cos_ref[...]
    sinf = sin_ref[...]
    n_out = w_ref.shape[1]
    for j in range(n_out // COL_CHUNK):
        c0 = j * COL_CHUNK
        z = jnp.dot(a, w_ref[:, c0:c0 + COL_CHUNK], preferred_element_type=F32)
        if c0 < 2 * ret_width:
            parts = []
            for hh in range(COL_CHUNK // HEAD_DIM):
                zh = z[:, hh * HEAD_DIM:(hh + 1) * HEAD_DIM]
                zh = zh * cosf + pltpu.roll(zh, HEAD_DIM // 2, axis=1) * sinf
                if c0 < ret_width:
                    zh = zh * qk_scale
                parts.append(zh)
            z = jnp.concatenate(parts, axis=1)
        z_ref[0, :, c0:c0 + COL_CHUNK] = z.astype(BF16)


def _ab_in_proj(x, ctx, modvec, g_mix, w_in, cosf, sinf, n_lat_blocks):
    b, t, d = x.shape
    rows = t + ctx.shape[1]
    n_out = w_in.shape[1]
    ret_width = n_out // 6
    kern = functools.partial(_ab_in_kernel, ret_width=ret_width, qk_scale=HEAD_DIM ** -0.5,
                             n_lat_blocks=n_lat_blocks)
    return pl.pallas_call(
        kern,
        grid=(b, rows // ROW_BLOCK),
        in_specs=[pl.BlockSpec((1, ROW_BLOCK, d), lambda bi, i: (bi, jnp.minimum(i, n_lat_blocks - 1), 0)),
                  pl.BlockSpec((1, ROW_BLOCK, d), lambda bi, i: (bi, 0, 0)),
                  pl.BlockSpec((1, 1, 6, d), lambda bi, i: (bi, i // n_lat_blocks, 0, 0)),
                  _resident((1, d)),
                  _resident((d, n_out)),
                  pl.BlockSpec((ROW_BLOCK, HEAD_DIM), lambda bi, i: (i, 0)),
                  pl.BlockSpec((ROW_BLOCK, HEAD_DIM), lambda bi, i: (i, 0))],
        out_specs=pl.BlockSpec((1, ROW_BLOCK, n_out), lambda bi, i: (bi, i, 0)),
        out_shape=jax.ShapeDtypeStruct((b, rows, n_out), BF16),
        compiler_params=_cparams(2),
        name="ab_in_proj",
    )(x, ctx, modvec, g_mix, w_in, cosf, sinf)


def _ret_state_kernel(k_ref, v_ref, kdec_ref, cdec_ref, r_ref, s_scr, *, n_chunks):
    s_scr[...] = jnp.zeros_like(s_scr)

    def body(i, carry):
        c = n_chunks - 1 - i
        rows = pl.ds(pl.multiple_of(c * CHUNK, CHUNK), CHUNK)
        for hh in range(s_scr.shape[0]):
            cols = slice(hh * HEAD_DIM, (hh + 1) * HEAD_DIM)
            r_ref[0, c, hh] = s_scr[hh].astype(BF16)
            kd = (k_ref[0, rows, cols].astype(F32) * kdec_ref[hh]).astype(BF16)
            u = lax.dot_general(kd, v_ref[0, rows, cols], (((0,), (0,)), ((), ())),
                                preferred_element_type=F32)
            s_scr[hh] = s_scr[hh] * cdec_ref[hh] + u
        return carry

    lax.fori_loop(0, n_chunks, body, 0)


def _ret_bwd_states(z_all, kdec_b, cdec_b, n_heads):
    b, rows, _ = z_all.shape
    n_chunks = rows // CHUNK
    hps = STATE_HEADS_PER_STEP
    wblk = hps * HEAD_DIM
    kern = functools.partial(_ret_state_kernel, n_chunks=n_chunks)
    return pl.pallas_call(
        kern,
        grid=(b, n_heads // hps),
        in_specs=[pl.BlockSpec((1, rows, wblk), lambda bi, g: (bi, 0, n_heads // hps + g)),
                  pl.BlockSpec((1, rows, wblk), lambda bi, g: (bi, 0, 2 * (n_heads // hps) + g)),
                  pl.BlockSpec((hps, CHUNK, HEAD_DIM), lambda bi, g: (g, 0, 0)),
                  pl.BlockSpec((hps, 1, HEAD_DIM), lambda bi, g: (g, 0, 0))],
        out_specs=pl.BlockSpec((1, n_chunks, hps, HEAD_DIM, HEAD_DIM), lambda bi, g: (bi, 0, g, 0, 0)),
        out_shape=jax.ShapeDtypeStruct((b, n_chunks, n_heads, HEAD_DIM, HEAD_DIM), BF16),
        scratch_shapes=[pltpu.VMEM((hps, HEAD_DIM, HEAD_DIM), F32)],
        compiler_params=_cparams(2),
        name="ret_bwd_states",
    )(z_all, z_all, kdec_b, cdec_b)


def _residual_ffn_in(x, m, mod, gffn, wr_ref, br_ref, xo_ref, f_ref, lg_ref):
    xn = x + mod[2:3] * m
    xo_ref[0] = xn
    f = _modulate(xn, gffn, mod[3:4], mod[4:5])
    f_ref[0] = f
    lg_ref[0] = jnp.dot(f.astype(BF16), wr_ref[...], preferred_element_type=F32) + br_ref[...]


def _ab_mix_kernel(z_ref, r_ref, x_ref, c_ref, mod_ref, wout_ref, ws_ref, bs_ref, dmat_ref, qdf_ref, qdb_ref,
                   kdf_ref, cdf_ref, gffn_ref, wr_ref, br_ref, xo_ref, f_ref, lg_ref, s_scr, rs_scr,
                   *, n_heads, n_sgu):
    @pl.when(pl.program_id(1) == 0)
    def _():
        s_scr[...] = jnp.zeros_like(s_scr)

    ret_w = n_heads * HEAD_DIM
    sgu_w = n_sgu * HEAD_DIM
    nt = (((1,), (1,)), ((), ()))
    tn = (((0,), (0,)), ((), ()))
    for c in range(ROW_BLOCK // CHUNK):
        r0 = c * CHUNK
        for h in range(n_heads):
            c0 = h * HEAD_DIM
            q = z_ref[0, r0:r0 + CHUNK, c0:c0 + HEAD_DIM]
            k = z_ref[0, r0:r0 + CHUNK, ret_w + c0:ret_w + c0 + HEAD_DIM]
            v = z_ref[0, r0:r0 + CHUNK, 2 * ret_w + c0:2 * ret_w + c0 + HEAD_DIM]
            g = z_ref[0, r0:r0 + CHUNK, 3 * ret_w + c0:3 * ret_w + c0 + HEAD_DIM].astype(F32)
            sc = lax.dot_general(q, k, nt, preferred_element_type=F32) * dmat_ref[h]
            y = jnp.dot(sc.astype(BF16), v, preferred_element_type=F32)
            y = y + jnp.dot(q, s_scr[h].astype(BF16), preferred_element_type=F32) * qdf_ref[h]
            y = y + jnp.dot(q, r_ref[0, c, h], preferred_element_type=F32) * qdb_ref[h]
            r = (g * _sigmoid(g)) * _standardise(y)
            rs_scr[r0:r0 + CHUNK, c0:c0 + HEAD_DIM] = r.astype(BF16)
            kd = (k.astype(F32) * kdf_ref[h]).astype(BF16)
            s_scr[h] = s_scr[h] * cdf_ref[h] + lax.dot_general(kd, v, tn, preferred_element_type=F32)
        for gi in range(n_sgu):
            c0 = 4 * ret_w + gi * HEAD_DIM
            u = _gelu_tanh(z_ref[0, r0:r0 + CHUNK, c0:c0 + HEAD_DIM].astype(F32))
            vv = _gelu_tanh(z_ref[0, r0:r0 + CHUNK, sgu_w + c0:sgu_w + c0 + HEAD_DIM].astype(F32))
            vn = _standardise(vv).astype(BF16)
            sg = jnp.dot(ws_ref[gi], vn, preferred_element_type=F32) + bs_ref[gi]
            rs_scr[r0:r0 + CHUNK, ret_w + gi * HEAD_DIM:ret_w + (gi + 1) * HEAD_DIM] = (u * sg).astype(BF16)
    m = jnp.dot(rs_scr[...], wout_ref[...], preferred_element_type=F32)
    rows_in = jnp.where(pl.program_id(1) == 0, c_ref[0], x_ref[0])
    _residual_ffn_in(rows_in, m, mod_ref[0, 0], gffn_ref[...], wr_ref, br_ref, xo_ref, f_ref, lg_ref)


def _ab_mix(z_all, r_states, x, ctx, modvec, w_out, w_s, b_s_rows, tabs, g_ffn, w_r, b_r, n_lat_blocks):
    b, t, d = x.shape
    rows = t + ctx.shape[1]
    n_out = z_all.shape[2]
    n_heads = r_states.shape[2]
    n_sgu = w_s.shape[0]
    nblk = rows // ROW_BLOCK
    cpb = ROW_BLOCK // CHUNK
    blk = lambda i: (i + n_lat_blocks) % nblk
    kern = functools.partial(_ab_mix_kernel, n_heads=n_heads, n_sgu=n_sgu)
    tab_spec = _resident((n_heads, CHUNK, HEAD_DIM))
    row_out = lambda w, dt: (pl.BlockSpec((1, ROW_BLOCK, w), lambda bi, i: (bi, blk(i), 0)),
                             jax.ShapeDtypeStruct((b, rows, w), dt))
    outs = [row_out(d, F32), row_out(d, F32), row_out(LANES, F32)]
    return pl.pallas_call(
        kern,
        grid=(b, nblk),
        in_specs=[pl.BlockSpec((1, ROW_BLOCK, n_out), lambda bi, i: (bi, blk(i), 0)),
                  pl.BlockSpec((1, cpb, n_heads, HEAD_DIM, HEAD_DIM), lambda bi, i: (bi, blk(i), 0, 0, 0)),
                  pl.BlockSpec((1, ROW_BLOCK, d), lambda bi, i: (bi, jnp.minimum(blk(i), n_lat_blocks - 1), 0)),
                  pl.BlockSpec((1, ROW_BLOCK, d), lambda bi, i: (bi, 0, 0)),
                  pl.BlockSpec((1, 1, 6, d), lambda bi, i: (bi, blk(i) // n_lat_blocks, 0, 0)),
                  _resident(w_out.shape),
                  _resident(w_s.shape),
                  _resident(b_s_rows.shape),
                  tab_spec, tab_spec, tab_spec, tab_spec,
                  _resident((n_heads, 1, HEAD_DIM)),
                  _resident((1, d)),
                  _resident(w_r.shape),
                  _resident((1, LANES))],
        out_specs=[o[0] for o in outs],
        out_shape=[o[1] for o in outs],
        scratch_shapes=[pltpu.VMEM((n_heads, HEAD_DIM, HEAD_DIM), F32),
                        pltpu.VMEM((ROW_BLOCK, d), BF16)],
        compiler_params=_cparams(2),
        name="ab_mix",
    )(z_all, r_states, x, ctx, modvec, w_out, w_s, b_s_rows, tabs["dmat"], tabs["qdf"], tabs["qdb"],
      tabs["kdf"], tabs["cdf"], g_ffn, w_r, b_r)


def _cv_in_kernel(x_ref, y_ref, mod0_ref, mod_ref, g_ref, w_ref, cw_ref, cb_ref, p_ref):
    mod = mod_ref[0, 0]
    x1 = x_ref[0] + mod0_ref[0, 0][5:6] * y_ref[...]
    a = _modulate(x1, g_ref[...], mod[0:1], mod[1:2]).astype(BF16)
    bm = a.shape[0]
    width = p_ref.shape[2]
    col = lax.broadcasted_iota(jnp.int32, (bm, COL_CHUNK), 0) % GRID_W
    for j in range(width // COL_CHUNK):
        c0 = j * COL_CHUNK
        gate_b = jnp.dot(a, w_ref[:, c0:c0 + COL_CHUNK], preferred_element_type=F32)
        gate_c = jnp.dot(a, w_ref[:, width + c0:width + c0 + COL_CHUNK], preferred_element_type=F32)
        hv = jnp.dot(a, w_ref[:, 2 * width + c0:2 * width + c0 + COL_CHUNK], preferred_element_type=F32)
        xg = gate_c * hv
        left = jnp.where(col == 0, 0.0, pltpu.roll(xg, 1, axis=0))
        right = jnp.where(col == GRID_W - 1, 0.0, pltpu.roll(xg, bm - 1, axis=0))
        y = cb_ref[:, c0:c0 + COL_CHUNK] + left * cw_ref[0:1, c0:c0 + COL_CHUNK]
        y = y + xg * cw_ref[1:2, c0:c0 + COL_CHUNK]
        y = y + right * cw_ref[2:3, c0:c0 + COL_CHUNK]
        p_ref[0, :, c0:c0 + COL_CHUNK] = (gate_b * y).astype(BF16)


def _cv_in_proj(x_mid, y_rows, modvec_prev, modvec, g_mix, w_in, conv_w, conv_b, n_lat_blocks):
    b, rows, d = x_mid.shape
    nblk = rows // ROW_BLOCK
    width = conv_w.shape[1]
    t = n_lat_blocks * ROW_BLOCK
    lat_mod = pl.BlockSpec((1, 1, 6, d), lambda bi, i: (bi, 0, 0, 0))
    return pl.pallas_call(
        _cv_in_kernel,
        grid=(b, n_lat_blocks),
        in_specs=[pl.BlockSpec((1, ROW_BLOCK, d), lambda bi, i: (bi, i, 0)),
                  pl.BlockSpec((ROW_BLOCK, d), lambda bi, i: (bi * nblk + i, 0)),
                  lat_mod, lat_mod,
                  _resident((1, d)),
                  _resident(w_in.shape),
                  _resident(conv_w.shape),
                  _resident((1, width))],
        out_specs=pl.BlockSpec((1, ROW_BLOCK, width), lambda bi, i: (bi, i, 0)),
        out_shape=jax.ShapeDtypeStruct((b, t, width), BF16),
        compiler_params=_cparams(2),
        name="cv_in_proj",
    )(x_mid, y_rows, modvec_prev, modvec, g_mix, w_in, conv_w, conv_b)


def _cv_out_kernel(p_ref, x_ref, y_ref, mod0_ref, mod_ref, wout_ref, gffn_ref, wr_ref, br_ref,
                   xo_ref, f_ref, lg_ref):
    x1 = x_ref[0] + mod0_ref[0, 0][5:6] * y_ref[...]
    m = jnp.dot(p_ref[0], wout_ref[...], preferred_element_type=F32)
    _residual_ffn_in(x1, m, mod_ref[0, 0], gffn_ref[...], wr_ref, br_ref, xo_ref, f_ref, lg_ref)


def _cv_out_proj(p, x_mid, y_rows, modvec_prev, modvec, w_out, g_ffn, w_r, b_r):
    b, t, width = p.shape
    d = w_out.shape[1]
    nblk = x_mid.shape[1] // ROW_BLOCK
    row_out = lambda w: (pl.BlockSpec((1, ROW_BLOCK, w), lambda bi, i: (bi, i, 0)),
                         jax.ShapeDtypeStruct((b, t, w), F32))
    outs = [row_out(d), row_out(d), row_out(LANES)]
    lat_mod = pl.BlockSpec((1, 1, 6, d), lambda bi, i: (bi, 0, 0, 0))
    return pl.pallas_call(
        _cv_out_kernel,
        grid=(b, t // ROW_BLOCK),
        in_specs=[pl.BlockSpec((1, ROW_BLOCK, width), lambda bi, i: (bi, i, 0)),
                  pl.BlockSpec((1, ROW_BLOCK, d), lambda bi, i: (bi, i, 0)),
                  pl.BlockSpec((ROW_BLOCK, d), lambda bi, i: (bi * nblk + i, 0)),
                  lat_mod, lat_mod,
                  _resident(w_out.shape),
                  _resident((1, d)),
                  _resident(w_r.shape),
                  _resident((1, LANES))],
        out_specs=[o[0] for o in outs],
        out_shape=[o[1] for o in outs],
        compiler_params=_cparams(2),
        name="cv_out_proj",
    )(p, x_mid, y_rows, modvec_prev, modvec, w_out, g_ffn, w_r, b_r)


def _route_kernel(lg_ref, info_ref, cnt_ref, cnt_scr):
    @pl.when(pl.program_id(0) == 0)
    def _():
        cnt_scr[...] = jnp.zeros_like(cnt_scr)

    lg = lg_ref[...]
    bm = lg.shape[0]
    lane = lax.broadcasted_iota(jnp.int32, lg.shape, 1).astype(F32)
    neg = -jnp.inf

    def first_max(vals):
        mx = jnp.max(vals, axis=1, keepdims=True)
        idx = jnp.min(jnp.where(vals == mx, lane, float(LANES)), axis=1, keepdims=True)
        return mx, idx

    is_grp = lane < N_GROUPS
    m1, grp = first_max(jnp.where(is_grp, lg, neg))
    den = jnp.sum(jnp.where(is_grp, jnp.exp(lg - m1), 0.0), axis=1, keepdims=True)
    p_grp = 1.0 / den
    base = N_GROUPS + EXPERTS_PER_GROUP * grp
    in_grp = (lane >= base) & (lane < base + EXPERTS_PER_GROUP)
    l2 = jnp.where(in_grp, lg, neg)
    v1, i1 = first_max(l2)
    v2, i2 = first_max(jnp.where(lane == i1, neg, l2))
    e = jnp.exp(v2 - v1)
    w_top1 = p_grp * (1.0 / (1.0 + e))
    w_top2 = p_grp * (e / (1.0 + e))
    first_lo = i1 < i2
    lo = jnp.where(first_lo, i1, i2) - base
    hi = jnp.where(first_lo, i2, i1) - base
    w_lo = jnp.where(first_lo, w_top1, w_top2)
    w_hi = jnp.where(first_lo, w_top2, w_top1)
    pair = jnp.where(lo == 0.0, jnp.where(hi == 1.0, 0.0, hi),
                     jnp.where(lo == 1.0, jnp.where(hi == 2.0, 1.0, 4.0), 5.0))
    w_a = jnp.where(pair == 0.0, w_lo, w_hi)
    w_b = jnp.where(pair == 0.0, w_hi, w_lo)
    cls = grp * N_PAIRS + pair

    onehot = (lane == cls).astype(F32)
    tri = (lax.broadcasted_iota(jnp.int32, (bm, bm), 0) > lax.broadcasted_iota(jnp.int32, (bm, bm), 1))
    before = jnp.dot(tri.astype(BF16), onehot.astype(BF16), preferred_element_type=F32)
    rank = jnp.sum(onehot * (before + cnt_scr[...]), axis=1, keepdims=True)
    cnt_scr[...] = cnt_scr[...] + jnp.sum(onehot, axis=0, keepdims=True)
    cnt_ref[...] = cnt_scr[...]
    info = jnp.where(lane == 0, cls,
                     jnp.where(lane == 1, rank,
                               jnp.where(lane == 2, w_a, jnp.where(lane == 3, w_b, 0.0))))
    info_ref[...] = info


def _route(logits):
    n = logits.shape[0]
    return pl.pallas_call(
        _route_kernel,
        grid=(n // ROW_BLOCK,),
        in_specs=[pl.BlockSpec((ROW_BLOCK, LANES), lambda i: (i, 0))],
        out_specs=[pl.BlockSpec((ROW_BLOCK, LANES), lambda i: (i, 0)),
                   pl.BlockSpec((1, LANES), lambda i: (0, 0))],
        out_shape=[jax.ShapeDtypeStruct((n, LANES), F32), jax.ShapeDtypeStruct((1, LANES), F32)],
        scratch_shapes=[pltpu.VMEM((1, LANES), F32)],
        compiler_params=_cparams(1),
        name="moe_route",
    )(logits)


def _dispatch_plan(info, counts, n_tok, expert_base):
    tm = MOE_TILE
    n_tiles = n_tok // tm + N_CLASSES
    n_rows = n_tiles * tm
    i32 = jnp.int32
    slot_a = jnp.asarray(PAIR_SLOT_A, i32)
    slot_b = jnp.asarray(PAIR_SLOT_B, i32)
    cls = info[:, 0].astype(i32)
    rank = info[:, 1].astype(i32)
    cnt = counts[0, :N_CLASSES].astype(i32)
    tiles_c = (cnt + tm - 1) // tm
    tile_end = jnp.cumsum(tiles_c)
    tile_start = tile_end - tiles_c
    pos = tile_start[cls] * tm + rank
    tok = jnp.arange(n_tok, dtype=F32)[:, None]
    per_row = jnp.zeros((n_rows, 3), F32).at[pos].set(jnp.concatenate([tok, info[:, 2:4]], axis=1))
    src = per_row[:, 0].astype(i32).reshape(n_tiles, tm)
    w_rows = per_row[:, 1:3]
    n_used = tile_end[-1]

    def class_of(step, ends):
        return jnp.minimum(jnp.sum((ends[None, :] <= step[:, None]).astype(i32), axis=1), N_CLASSES - 1)

    tile = jnp.arange(n_tiles, dtype=i32)
    tcls = class_of(jnp.minimum(tile, n_used - 1), tile_end)
    e_a = expert_base + (tcls // N_PAIRS) * EXPERTS_PER_GROUP + slot_a[tcls % N_PAIRS]
    e_b = expert_base + (tcls // N_PAIRS) * EXPERTS_PER_GROUP + slot_b[tcls % N_PAIRS]
    n_valid = jnp.clip(cnt[tcls] - (tile - tile_start[tcls]) * tm, 0, tm)
    n_valid = jnp.where(tile < n_used, n_valid, 0).astype(i32)
    row = jnp.arange(tm, dtype=i32)[None, :]
    dump = n_tok + (tile % 2)[:, None] * tm + row
    dst = jnp.where(row < n_valid[:, None], src, dump)

    return dict(src=src.reshape(n_tiles, 1, tm), dst=dst.reshape(n_tiles, 1, tm), w_rows=w_rows,
                n_valid=n_valid, n_tiles=n_tiles, a_runs=_weight_runs(e_a, n_used),
                b_runs=_weight_runs(e_b, n_used), e_a=e_a.astype(i32), e_b=e_b.astype(i32))


def _weight_runs(expert, n_real):
    i32 = jnp.int32
    n = expert.shape[0]
    idx = jnp.arange(n, dtype=i32)
    prev = jnp.concatenate([jnp.full((1,), -1, i32), expert[:-1].astype(i32)])
    first = ((expert != prev) & (idx < n_real)).astype(i32)
    parity = (jnp.cumsum(first) - 1) % 2
    later = (idx[None, :] > idx[:, None]) & (first[None, :] == 1)
    nxt_idx = jnp.min(jnp.where(later, idx[None, :], n), axis=1)
    nxt = jnp.where(nxt_idx < n, expert[jnp.minimum(nxt_idx, n - 1)], -1)
    return first, parity.astype(i32), nxt.astype(i32)


def _row_copy(src_hbm, src_row, dst_buf, dst_row, sem):
    return pltpu.make_async_copy(src_hbm.at[pl.ds(src_row, 1), :], dst_buf.at[pl.ds(dst_row, 1), :], sem)


def _cast_rows(src, dst):
    def body(c, carry):
        rows = pl.ds(pl.multiple_of(c * CAST_ROWS, CAST_ROWS), CAST_ROWS)
        dst[rows, :] = src[rows, :].astype(BF16)
        return carry
    lax.fori_loop(0, src.shape[0] // CAST_ROWS, body, 0)


def _moe_up_kernel(nv_ref, ea_ref, fa_ref, na_ref, eb_ref, fb_ref, nb_ref,
                   src_ref, nsrc_ref, f_hbm, ws_ref, w1_hbm, w3_hbm, h_ref,
                   xbuf, xsem, wf32, wbf, wsem):
    i = pl.program_id(0)
    tm = xbuf.shape[1]
    xs = i % 2
    half = wbf.shape[3]
    runs = ((ea_ref, fa_ref, na_ref), (eb_ref, fb_ref, nb_ref))
    used = lambda tile: nv_ref[jnp.minimum(tile, pl.num_programs(0) - 1)] > 0

    def fetch(s, expert):
        pltpu.make_async_copy(w1_hbm.at[expert], wf32.at[s, 0], wsem.at[s]).start()
        pltpu.make_async_copy(w3_hbm.at[expert], wf32.at[s, 1], wsem.at[s]).start()

    def gather(idx_ref, s):
        for r in range(tm):
            _row_copy(f_hbm, idx_ref[0, 0, r], xbuf.at[s], r, xsem.at[s]).start()

    @pl.when(i == 0)
    def _():
        for s, (e_ref, _, _) in enumerate(runs):
            fetch(s, e_ref[0])
        gather(src_ref, 0)

    @pl.when(used(i))
    def _():
        for s, (_, f_ref, n_ref) in enumerate(runs):
            @pl.when(f_ref[i] == 1)
            def _():
                for m in range(2):
                    pltpu.make_async_copy(w1_hbm.at[0], wf32.at[s, m], wsem.at[s]).wait()
                for m in range(2):
                    _cast_rows(wf32.at[s, m], wbf.at[s, m])

                @pl.when(n_ref[i] >= 0)
                def _():
                    fetch(s, n_ref[i])

        @pl.when((i + 1 < pl.num_programs(0)) & used(i + 1))
        def _():
            gather(nsrc_ref, 1 - xs)

        pltpu.make_async_copy(f_hbm.at[pl.ds(0, tm), :], xbuf.at[xs], xsem.at[xs]).wait()
        x = xbuf[xs].astype(BF16)
        ws = ws_ref[...]
        for s in range(2):
            up = jnp.dot(x, wbf[s, 0], preferred_element_type=F32)
            gate = jnp.dot(x, wbf[s, 1], preferred_element_type=F32)
            act = (up * _sigmoid(up)) * gate
            h_ref[:, s * half:(s + 1) * half] = (act * ws[:, s:s + 1]).astype(BF16)

    @pl.when(jnp.logical_not(used(i)))
    def _():
        h_ref[...] = jnp.zeros_like(h_ref)


def _moe_up(f_rows, plan, w1, w3):
    n_tiles = plan["n_tiles"]
    tm = MOE_TILE
    d = f_rows.shape[1]
    n_e, _, de = w1.shape
    last = n_tiles - 1
    (fa, _, na), (fb, _, nb) = plan["a_runs"], plan["b_runs"]
    grid_spec = pltpu.PrefetchScalarGridSpec(
        num_scalar_prefetch=7,
        grid=(n_tiles,),
        in_specs=[pl.BlockSpec((1, 1, tm), lambda i, *_: (i, 0, 0), memory_space=pltpu.SMEM),
                  pl.BlockSpec((1, 1, tm), lambda i, *_: (jnp.minimum(i + 1, last), 0, 0),
                               memory_space=pltpu.SMEM),
                  pl.BlockSpec(memory_space=pl.ANY),
                  pl.BlockSpec((tm, 2), lambda i, *_: (i, 0)),
                  pl.BlockSpec(memory_space=pl.ANY),
                  pl.BlockSpec(memory_space=pl.ANY)],
        out_specs=pl.BlockSpec((tm, 2 * de), lambda i, *_: (i, 0)),
        scratch_shapes=[pltpu.VMEM((2, tm, d), F32), pltpu.SemaphoreType.DMA((2,)),
                        pltpu.VMEM((2, 2, d, de), F32), pltpu.VMEM((2, 2, d, de), BF16),
                        pltpu.SemaphoreType.DMA((2,))],
    )
    return pl.pallas_call(
        _moe_up_kernel,
        grid_spec=grid_spec,
        out_shape=jax.ShapeDtypeStruct((n_tiles * tm, 2 * de), BF16),
        compiler_params=pltpu.CompilerParams(dimension_semantics=("arbitrary",),
                                             vmem_limit_bytes=MOE_UP_VMEM_LIMIT),
        name="moe_up",
    )(plan["n_valid"], plan["e_a"], fa, na, plan["e_b"], fb, nb,
      plan["src"], plan["src"], f_rows, plan["w_rows"], w1, w3)


def _moe_down_kernel(nv_ref, ea_ref, fa_ref, pa_ref, na_ref, eb_ref, fb_ref, pb_ref, nb_ref,
                     dst_ref, h_ref, w2_hbm, y_hbm, obuf, sems, wf32, wbf, wsem):
    i = pl.program_id(0)
    n_steps = pl.num_programs(0)
    slot = i % 2
    half = wbf.shape[1]
    tm = obuf.shape[1]
    runs = ((ea_ref, fa_ref, pa_ref, na_ref), (eb_ref, fb_ref, pb_ref, nb_ref))

    def fetch(s, expert, buf):
        pltpu.make_async_copy(w2_hbm.at[expert], wf32.at[s, buf], wsem.at[s, buf]).start()

    def drain(tile, s):
        @pl.when(nv_ref[tile] > 0)
        def _():
            pltpu.make_async_copy(obuf.at[s], y_hbm.at[pl.ds(0, tm), :], sems.at[s]).wait()

    @pl.when(i == 0)
    def _():
        n_tok = y_hbm.shape[0] - 2 * tm
        obuf[...] = jnp.zeros_like(obuf)
        for s in range(2):
            fill = pltpu.make_async_copy(obuf.at[s], y_hbm.at[pl.ds(n_tok + s * tm, tm), :], sems.at[s])
            fill.start()
            fill.wait()
        for s, (e_ref, _, _, _) in enumerate(runs):
            fetch(s, e_ref[0], 0)

    @pl.when(i >= 2)
    def _():
        drain(i - 2, slot)

    @pl.when(nv_ref[i] > 0)
    def _():
        for s, (_, f_ref, p_ref, n_ref) in enumerate(runs):
            @pl.when(f_ref[i] == 1)
            def _():
                buf = p_ref[i]
                pltpu.make_async_copy(w2_hbm.at[0], wf32.at[s, buf], wsem.at[s, buf]).wait()

                @pl.when(n_ref[i] >= 0)
                def _():
                    fetch(s, n_ref[i], 1 - buf)
                _cast_rows(wf32.at[s, buf], wbf.at[s])

        out = jnp.dot(h_ref[:, :half], wbf[0], preferred_element_type=F32)
        out = out + jnp.dot(h_ref[:, half:], wbf[1], preferred_element_type=F32)
        obuf[slot] = out
        for r in range(tm):
            _row_copy(obuf.at[slot], r, y_hbm, dst_ref[0, 0, r], sems.at[slot]).start()

    @pl.when(i == n_steps - 1)
    def _():
        @pl.when(i >= 1)
        def _():
            drain(i - 1, 1 - slot)
        drain(i, slot)


def _moe_down(h_rows, plan, w2, n_tok):
    n_tiles = plan["n_tiles"]
    tm = MOE_TILE
    n_e, de, d = w2.shape
    grid_spec = pltpu.PrefetchScalarGridSpec(
        num_scalar_prefetch=9,
        grid=(n_tiles,),
        in_specs=[pl.BlockSpec((1, 1, tm), lambda i, *_: (i, 0, 0), memory_space=pltpu.SMEM),
                  pl.BlockSpec((tm, 2 * de), lambda i, *_: (i, 0)),
                  pl.BlockSpec(memory_space=pl.ANY)],
        out_specs=pl.BlockSpec(memory_space=pl.ANY),
        scratch_shapes=[pltpu.VMEM((2, tm, d), F32), pltpu.SemaphoreType.DMA((2,)),
                        pltpu.VMEM((2, 2, de, d), F32), pltpu.VMEM((2, de, d), BF16),
                        pltpu.SemaphoreType.DMA((2, 2))],
    )
    return pl.pallas_call(
        _moe_down_kernel,
        grid_spec=grid_spec,
        out_shape=jax.ShapeDtypeStruct((n_tok + 2 * tm, d), F32),
        compiler_params=_cparams(1),
        name="moe_down",
    )(plan["n_valid"], plan["e_a"], *plan["a_runs"], plan["e_b"], *plan["b_runs"], plan["dst"], h_rows, w2)


def _hier_moe(f_rows, logits, w1, w3, w2, layer):
    n_tok = f_rows.shape[0]
    n_layers, n_exp = w1.shape[:2]
    merged = lambda w: w.reshape((n_layers * n_exp,) + w.shape[2:])
    info, counts = _route(logits)
    plan = _dispatch_plan(info, counts, n_tok, layer * n_exp)
    h_rows = _moe_up(f_rows, plan, merged(w1), merged(w3))
    return _moe_down(h_rows, plan, merged(w2), n_tok)


def _resid_kernel(x_ref, y_ref, mod_ref, o_ref):
    o_ref[0] = x_ref[0] + mod_ref[0, 0][5:6] * y_ref[...]


def _final_kernel(x_ref, y_ref, mod_ref, g_ref, o_ref):
    x = x_ref[0] + mod_ref[0, 0][5:6] * y_ref[...]
    ms = jnp.mean(x * x, axis=-1, keepdims=True)
    o_ref[0] = (x * lax.rsqrt(ms + EPS)) * g_ref[...]


def _gated_residual(x, y_rows, modvec, first_block, n_blocks, mod_row, g_final=None):
    b, rows, d = x.shape
    nblk = rows // ROW_BLOCK
    in_specs = [pl.BlockSpec((1, ROW_BLOCK, d), lambda bi, i: (bi, first_block + i, 0)),
                pl.BlockSpec((ROW_BLOCK, d), lambda bi, i: (bi * nblk + first_block + i, 0)),
                pl.BlockSpec((1, 1, 6, d), lambda bi, i: (bi, mod_row, 0, 0))]
    args = [x, y_rows, modvec]
    kern = _resid_kernel
    if g_final is not None:
        in_specs.append(_resident((1, d)))
        args.append(g_final)
        kern = _final_kernel
    return pl.pallas_call(
        kern,
        grid=(b, n_blocks),
        in_specs=in_specs,
        out_specs=pl.BlockSpec((1, ROW_BLOCK, d), lambda bi, i: (bi, i, 0)),
        out_shape=jax.ShapeDtypeStruct((b, n_blocks * ROW_BLOCK, d), F32),
        compiler_params=_cparams(2),
        name="gated_residual" if g_final is None else "final_norm",
    )(*args)


def _rope_tables(t, ctx_len):
    pos = jnp.arange(t)
    row = (pos // GRID_W).astype(F32)
    col = (pos % GRID_W).astype(F32)
    n_freq = HEAD_DIM // 4
    inv = ROPE_BASE ** (-jnp.arange(n_freq, dtype=F32) / n_freq)
    ang = jnp.concatenate([row[:, None] * inv, col[:, None] * inv], axis=-1)
    cos, sin = jnp.cos(ang), jnp.sin(ang)
    cosf = jnp.concatenate([cos, cos], axis=-1)
    sinf = jnp.concatenate([-sin, sin], axis=-1)
    cosf = jnp.concatenate([cosf, jnp.ones((ctx_len, HEAD_DIM), F32)], axis=0)
    sinf = jnp.concatenate([sinf, jnp.zeros((ctx_len, HEAD_DIM), F32)], axis=0)
    return cosf, sinf


def _retention_tables(decay_logit):
    lg_f = jax.nn.log_sigmoid(decay_logit[0].astype(F32))
    lg_b = jax.nn.log_sigmoid(decay_logit[1].astype(F32))
    n_heads = lg_f.shape[0]
    i = jnp.arange(CHUNK, dtype=F32)
    diff = i[:, None] - i[None, :]
    fwd = jnp.exp(jnp.where(diff >= 0, diff[None] * lg_f[:, None, None], -jnp.inf))
    bwd = jnp.exp(jnp.where(diff < 0, -diff[None] * lg_b[:, None, None], -jnp.inf))
    rows = lambda v: jnp.broadcast_to(v[:, :, None], (n_heads, CHUNK, HEAD_DIM))
    return dict(
        dmat=fwd + bwd,
        qdf=rows(jnp.exp((i + 1.0)[None, :] * lg_f[:, None])),
        qdb=rows(jnp.exp((CHUNK - i)[None, :] * lg_b[:, None])),
        kdf=rows(jnp.exp((CHUNK - 1.0 - i)[None, :] * lg_f[:, None])),
        kdb=rows(jnp.exp(i[None, :] * lg_b[:, None])),
        cdf=jnp.broadcast_to(jnp.exp(CHUNK * lg_f)[:, None, None], (n_heads, 1, HEAD_DIM)),
        cdb=jnp.broadcast_to(jnp.exp(CHUNK * lg_b)[:, None, None], (n_heads, 1, HEAD_DIM)),
    )


def _router_weights(w_r1, b_r1, w_r2, b_r2):
    d = w_r1.shape[0]
    n_exp = N_GROUPS * EXPERTS_PER_GROUP
    w = jnp.concatenate([w_r1, w_r2.transpose(1, 0, 2).reshape(d, n_exp)], axis=1)
    b = jnp.concatenate([b_r1, b_r2.reshape(n_exp)])
    pad = LANES - w.shape[1]
    return jnp.pad(w, ((0, 0), (0, pad))).astype(BF16), jnp.pad(b, (0, pad)).reshape(1, LANES)


def kernel(x, c, ctx, c_ctx, w_mod, b_mod, g_mix, g_ffn, ab_w_in, ab_w_out, ret_decay_logit, sgu_w_s, sgu_b_s,
           cv_w_in, cv_conv_w, cv_conv_b, cv_w_out, moe_w_r1, moe_b_r1, moe_w_r2, moe_b_r2, moe_w1, moe_w3,
           moe_w2, g_final):
    b, t, d = x.shape
    ctx_len = ctx.shape[1]
    n_lat_blocks = t // ROW_BLOCK
    n_heads = ret_decay_logit.shape[2]

    cvec = jnp.zeros((8, d), F32).at[:b].set(c).at[b].set(c_ctx)
    mod = _mod_vectors(cvec, w_mod, b_mod)
    mod_lat = mod[:, :b].reshape(2, b, 1, 6, d)
    mod_ctx = jnp.broadcast_to(mod[:, b].reshape(2, 1, 1, 6, d), (2, b, 1, 6, d))
    modvec = jnp.concatenate([mod_lat, mod_ctx], axis=2)

    assert ctx_len == ROW_BLOCK and t % ROW_BLOCK == 0
    cosf, sinf = _rope_tables(t, ctx_len)
    tabs = _retention_tables(ret_decay_logit[0])
    z_all = _ab_in_proj(x, ctx, modvec[0], g_mix[0:1], ab_w_in[0].astype(BF16), cosf, sinf, n_lat_blocks)
    r_states = _ret_bwd_states(z_all, tabs["kdb"], tabs["cdb"], n_heads)
    b_s_rows = jnp.broadcast_to(sgu_b_s[0][:, :, None], sgu_b_s[0].shape + (HEAD_DIM,))
    w_r, b_r = _router_weights(moe_w_r1[0], moe_b_r1[0], moe_w_r2[0], moe_b_r2[0])
    x_mid, f0, lg0 = _ab_mix(z_all, r_states, x, ctx, modvec[0], ab_w_out[0].astype(BF16),
                             sgu_w_s[0].astype(BF16), b_s_rows, tabs, g_ffn[0:1], w_r, b_r, n_lat_blocks)
    n0 = b * (t + ctx_len)
    y0 = _hier_moe(f0.reshape(n0, d), lg0.reshape(n0, LANES), moe_w1, moe_w3, moe_w2, 0)
    ctx_out = _gated_residual(x_mid, y0, modvec[0], n_lat_blocks, ctx_len // ROW_BLOCK, 1)
    del ctx_out

    p = _cv_in_proj(x_mid, y0, modvec[0], modvec[1], g_mix[1:2], cv_w_in[0].astype(BF16), cv_conv_w[0],
                    cv_conv_b[0:1], n_lat_blocks)
    w_r, b_r = _router_weights(moe_w_r1[1], moe_b_r1[1], moe_w_r2[1], moe_b_r2[1])
    x2, f1, lg1 = _cv_out_proj(p, x_mid, y0, modvec[0], modvec[1], cv_w_out[0].astype(BF16), g_ffn[1:2],
                               w_r, b_r)
    y1 = _hier_moe(f1.reshape(b * t, d), lg1.reshape(b * t, LANES), moe_w1, moe_w3, moe_w2, 1)
    return _gated_residual(x2, y1, modvec[1], 0, n_lat_blocks, 0, g_final.reshape(1, d))
```

```python
import functools

import jax
import jax.numpy as jnp
import numpy as np
from jax import lax
from jax.experimental import pallas as pl
from jax.experimental.pallas import tpu as pltpu

F32 = jnp.float32
BF16 = jnp.bfloat16

EPS = 1e-6
GRID_W = 64
HEAD_DIM = 128
CHUNK = 128
ROPE_BASE = 10000.0
N_GROUPS = 4
EXPERTS_PER_GROUP = 4
N_PAIRS = 6
N_CLASSES = N_GROUPS * N_PAIRS
PAIR_SLOT_A = (0, 2, 2, 3, 3, 3)
PAIR_SLOT_B = (1, 1, 0, 0, 1, 2)
LANES = 128
ROW_BLOCK = 256
MOE_TILE = 128
COL_CHUNK = 512
STATE_HEADS_PER_STEP = 4
CAST_ROWS = 256
VMEM_LIMIT = 56 * 1024 * 1024
MOE_UP_VMEM_LIMIT = 60 * 1024 * 1024
GELU_C = float(np.float32(np.sqrt(2.0 / np.pi)))


def _cparams(n_axes):
    return pltpu.CompilerParams(dimension_semantics=("arbitrary",) * n_axes,
                                vmem_limit_bytes=VMEM_LIMIT)


def _resident(shape):
    zeros = (0,) * len(shape)
    return pl.BlockSpec(shape, lambda *_: zeros, pipeline_mode=pl.Buffered(1))


def _sigmoid(x):
    return 1.0 / (1.0 + jnp.exp(-x))


def _gelu_tanh(x):
    return x * (0.5 * (1.0 + jnp.tanh(GELU_C * (x + 0.044715 * (x * x * x)))))


def _modulate(x, g, shift, scale):
    ms = jnp.mean(x * x, axis=-1, keepdims=True)
    return (x * lax.rsqrt(ms + EPS) * g) * (1.0 + scale) + shift


def _standardise(y):
    mu = jnp.mean(y, axis=-1, keepdims=True)
    d = y - mu
    var = jnp.mean(d * d, axis=-1, keepdims=True)
    return d * lax.rsqrt(var + EPS)


def _mod_kernel(c_ref, w_ref, b_ref, o_ref):
    c = c_ref[...]
    s = (c * _sigmoid(c)).astype(BF16)
    o_ref[0] = jnp.dot(s, w_ref[0].astype(BF16), preferred_element_type=F32) + b_ref[0]


def _mod_vectors(cvec, w_mod, b_mod):
    depth, d, n = w_mod.shape
    bn = 1536
    return pl.pallas_call(
        _mod_kernel,
        grid=(depth, n // bn),
        in_specs=[pl.BlockSpec((8, d), lambda l, j: (0, 0)),
                  pl.BlockSpec((1, d, bn), lambda l, j: (l, 0, j)),
                  pl.BlockSpec((1, 1, bn), lambda l, j: (l, 0, j))],
        out_specs=pl.BlockSpec((1, 8, bn), lambda l, j: (l, 0, j)),
        out_shape=jax.ShapeDtypeStruct((depth, 8, n), F32),
        compiler_params=_cparams(2),
        name="mod_vectors",
    )(cvec, w_mod, b_mod.reshape(depth, 1, n))


def _ab_in_kernel(x_ref, c_ref, mod_ref, g_ref, w_ref, cos_ref, sin_ref, z_ref, *, ret_width, qk_scale,
                  n_lat_blocks):
    mod = mod_ref[0, 0]
    rows_in = jnp.where(pl.program_id(1) < n_lat_blocks, x_ref[0], c_ref[0])
    a = _modulate(rows_in, g_ref[...], mod[0:1], mod[1:2]).astype(BF16)
    cosf = cos_ref[...]
    sinf = sin_ref[...]
    n_out = w_ref.shape[1]
    for j in range(n_out // COL_CHUNK):
        c0 = j * COL_CHUNK
        z = jnp.dot(a, w_ref[:, c0:c0 + COL_CHUNK], preferred_element_type=F32)
        if c0 < 2 * ret_width:
            parts = []
            for hh in range(COL_CHUNK // HEAD_DIM):
                zh = z[:, hh * HEAD_DIM:(hh + 1) * HEAD_DIM]
                zh = zh * cosf + pltpu.roll(zh, HEAD_DIM // 2, axis=1) * sinf
                if c0 < ret_width:
                    zh = zh * qk_scale
                parts.append(zh)
            z = jnp.concatenate(parts, axis=1)
        z_ref[0, :, c0:c0 + COL_CHUNK] = z.astype(BF16)


def _ab_in_proj(x, ctx, modvec, g_mix, w_in, cosf, sinf, n_lat_blocks):
    b, t, d = x.shape
    rows = t + ctx.shape[1]
    n_out = w_in.shape[1]
    ret_width = n_out // 6
    kern = functools.partial(_ab_in_kernel, ret_width=ret_width, qk_scale=HEAD_DIM ** -0.5,
                             n_lat_blocks=n_lat_blocks)
    return pl.pallas_call(
        kern,
        grid=(b, rows // ROW_BLOCK),
        in_specs=[pl.BlockSpec((1, ROW_BLOCK, d), lambda bi, i: (bi, jnp.minimum(i, n_lat_blocks - 1), 0)),
                  pl.BlockSpec((1, ROW_BLOCK, d), lambda bi, i: (bi, 0, 0)),
                  pl.BlockSpec((1, 1, 6, d), lambda bi, i: (bi, i // n_lat_blocks, 0, 0)),
                  _resident((1, d)),
                  _resident((d, n_out)),
                  pl.BlockSpec((ROW_BLOCK, HEAD_DIM), lambda bi, i: (i, 0)),
                  pl.BlockSpec((ROW_BLOCK, HEAD_DIM), lambda bi, i: (i, 0))],
        out_specs=pl.BlockSpec((1, ROW_BLOCK, n_out), lambda bi, i: (bi, i, 0)),
        out_shape=jax.ShapeDtypeStruct((b, rows, n_out), BF16),
        compiler_params=_cparams(2),
        name="ab_in_proj",
    )(x, ctx, modvec, g_mix, w_in, cosf, sinf)


def _ret_state_kernel(k_ref, v_ref, kdec_ref, cdec_ref, r_ref, s_scr, *, n_chunks):
    s_scr[...] = jnp.zeros_like(s_scr)

    def body(i, carry):
        c = n_chunks - 1 - i
        rows = pl.ds(pl.multiple_of(c * CHUNK, CHUNK), CHUNK)
        for hh in range(s_scr.shape[0]):
            cols = slice(hh * HEAD_DIM, (hh + 1) * HEAD_DIM)
            r_ref[0, c, hh] = s_scr[hh].astype(BF16)
            kd = (k_ref[0, rows, cols].astype(F32) * kdec_ref[hh]).astype(BF16)
            u = lax.dot_general(kd, v_ref[0, rows, cols], (((0,), (0,)), ((), ())),
                                preferred_element_type=F32)
            s_scr[hh] = s_scr[hh] * cdec_ref[hh] + u
        return carry

    lax.fori_loop(0, n_chunks, body, 0)


def _ret_bwd_states(z_all, kdec_b, cdec_b, n_heads):
    b, rows, _ = z_all.shape
    n_chunks = rows // CHUNK
    hps = STATE_HEADS_PER_STEP
    wblk = hps * HEAD_DIM
    kern = functools.partial(_ret_state_kernel, n_chunks=n_chunks)
    return pl.pallas_call(
        kern,
        grid=(b, n_heads // hps),
        in_specs=[pl.BlockSpec((1, rows, wblk), lambda bi, g: (bi, 0, n_heads // hps + g)),
                  pl.BlockSpec((1, rows, wblk), lambda bi, g: (bi, 0, 2 * (n_heads // hps) + g)),
                  pl.BlockSpec((hps, CHUNK, HEAD_DIM), lambda bi, g: (g, 0, 0)),
                  pl.BlockSpec((hps, 1, HEAD_DIM), lambda bi, g: (g, 0, 0))],
        out_specs=pl.BlockSpec((1, n_chunks, hps, HEAD_DIM, HEAD_DIM), lambda bi, g: (bi, 0, g, 0, 0)),
        out_shape=jax.ShapeDtypeStruct((b, n_chunks, n_heads, HEAD_DIM, HEAD_DIM), BF16),
        scratch_shapes=[pltpu.VMEM((hps, HEAD_DIM, HEAD_DIM), F32)],
        compiler_params=_cparams(2),
        name="ret_bwd_states",
    )(z_all, z_all, kdec_b, cdec_b)


def _residual_ffn_in(x, m, mod, gffn, wr_ref, br_ref, xo_ref, f_ref, lg_ref):
    xn = x + mod[2:3] * m
    xo_ref[0] = xn
    f = _modulate(xn, gffn, mod[3:4], mod[4:5])
    f_ref[0] = f
    lg_ref[0] = jnp.dot(f.astype(BF16), wr_ref[...], preferred_element_type=F32) + br_ref[...]


def _ab_mix_kernel(z_ref, r_ref, x_ref, c_ref, mod_ref, wout_ref, ws_ref, bs_ref, dmat_ref, qdf_ref, qdb_ref,
                   kdf_ref, cdf_ref, gffn_ref, wr_ref, br_ref, xo_ref, f_ref, lg_ref, s_scr, rs_scr,
                   *, n_heads, n_sgu):
    @pl.when(pl.program_id(1) == 0)
    def _():
        s_scr[...] = jnp.zeros_like(s_scr)

    ret_w = n_heads * HEAD_DIM
    sgu_w = n_sgu * HEAD_DIM
    nt = (((1,), (1,)), ((), ()))
    tn = (((0,), (0,)), ((), ()))
    for c in range(ROW_BLOCK // CHUNK):
        r0 = c * CHUNK
        for h in range(n_heads):
            c0 = h * HEAD_DIM
            q = z_ref[0, r0:r0 + CHUNK, c0:c0 + HEAD_DIM]
            k = z_ref[0, r0:r0 + CHUNK, ret_w + c0:ret_w + c0 + HEAD_DIM]
            v = z_ref[0, r0:r0 + CHUNK, 2 * ret_w + c0:2 * ret_w + c0 + HEAD_DIM]
            g = z_ref[0, r0:r0 + CHUNK, 3 * ret_w + c0:3 * ret_w + c0 + HEAD_DIM].astype(F32)
            sc = lax.dot_general(q, k, nt, preferred_element_type=F32) * dmat_ref[h]
            y = jnp.dot(sc.astype(BF16), v, preferred_element_type=F32)
            y = y + jnp.dot(q, s_scr[h].astype(BF16), preferred_element_type=F32) * qdf_ref[h]
            y = y + jnp.dot(q, r_ref[0, c, h], preferred_element_type=F32) * qdb_ref[h]
            r = (g * _sigmoid(g)) * _standardise(y)
            rs_scr[r0:r0 + CHUNK, c0:c0 + HEAD_DIM] = r.astype(BF16)
            kd = (k.astype(F32) * kdf_ref[h]).astype(BF16)
            s_scr[h] = s_scr[h] * cdf_ref[h] + lax.dot_general(kd, v, tn, preferred_element_type=F32)
        for gi in range(n_sgu):
            c0 = 4 * ret_w + gi * HEAD_DIM
            u = _gelu_tanh(z_ref[0, r0:r0 + CHUNK, c0:c0 + HEAD_DIM].astype(F32))
            vv = _gelu_tanh(z_ref[0, r0:r0 + CHUNK, sgu_w + c0:sgu_w + c0 + HEAD_DIM].astype(F32))
            vn = _standardise(vv).astype(BF16)
            sg = jnp.dot(ws_ref[gi], vn, preferred_element_type=F32) + bs_ref[gi]
            rs_scr[r0:r0 + CHUNK, ret_w + gi * HEAD_DIM:ret_w + (gi + 1) * HEAD_DIM] = (u * sg).astype(BF16)
    m = jnp.dot(rs_scr[...], wout_ref[...], preferred_element_type=F32)
    rows_in = jnp.where(pl.program_id(1) == 0, c_ref[0], x_ref[0])
    _residual_ffn_in(rows_in, m, mod_ref[0, 0], gffn_ref[...], wr_ref, br_ref, xo_ref, f_ref, lg_ref)


def _ab_mix(z_all, r_states, x, ctx, modvec, w_out, w_s, b_s_rows, tabs, g_ffn, w_r, b_r, n_lat_blocks):
    b, t, d = x.shape
    rows = t + ctx.shape[1]
    n_out = z_all.shape[2]
    n_heads = r_states.shape[2]
    n_sgu = w_s.shape[0]
    nblk = rows // ROW_BLOCK
    cpb = ROW_BLOCK // CHUNK
    blk = lambda i: (i + n_lat_blocks) % nblk
    kern = functools.partial(_ab_mix_kernel, n_heads=n_heads, n_sgu=n_sgu)
    tab_spec = _resident((n_heads, CHUNK, HEAD_DIM))
    row_out = lambda w, dt: (pl.BlockSpec((1, ROW_BLOCK, w), lambda bi, i: (bi, blk(i), 0)),
                             jax.ShapeDtypeStruct((b, rows, w), dt))
    outs = [row_out(d, F32), row_out(d, F32), row_out(LANES, F32)]
    return pl.pallas_call(
        kern,
        grid=(b, nblk),
        in_specs=[pl.BlockSpec((1, ROW_BLOCK, n_out), lambda bi, i: (bi, blk(i), 0)),
                  pl.BlockSpec((1, cpb, n_heads, HEAD_DIM, HEAD_DIM), lambda bi, i: (bi, blk(i), 0, 0, 0)),
                  pl.BlockSpec((1, ROW_BLOCK, d), lambda bi, i: (bi, jnp.minimum(blk(i), n_lat_blocks - 1), 0)),
                  pl.BlockSpec((1, ROW_BLOCK, d), lambda bi, i: (bi, 0, 0)),
                  pl.BlockSpec((1, 1, 6, d), lambda bi, i: (bi, blk(i) // n_lat_blocks, 0, 0)),
                  _resident(w_out.shape),
                  _resident(w_s.shape),
                  _resident(b_s_rows.shape),
                  tab_spec, tab_spec, tab_spec, tab_spec,
                  _resident((n_heads, 1, HEAD_DIM)),
                  _resident((1, d)),
                  _resident(w_r.shape),
                  _resident((1, LANES))],
        out_specs=[o[0] for o in outs],
        out_shape=[o[1] for o in outs],
        scratch_shapes=[pltpu.VMEM((n_heads, HEAD_DIM, HEAD_DIM), F32),
                        pltpu.VMEM((ROW_BLOCK, d), BF16)],
        compiler_params=_cparams(2),
        name="ab_mix",
    )(z_all, r_states, x, ctx, modvec, w_out, w_s, b_s_rows, tabs["dmat"], tabs["qdf"], tabs["qdb"],
      tabs["kdf"], tabs["cdf"], g_ffn, w_r, b_r)


def _cv_in_kernel(x_ref, y_ref, mod0_ref, mod_ref, g_ref, w_ref, cw_ref, cb_ref, p_ref):
    mod = mod_ref[0, 0]
    x1 = x_ref[0] + mod0_ref[0, 0][5:6] * y_ref[...]
    a = _modulate(x1, g_ref[...], mod[0:1], mod[1:2]).astype(BF16)
    bm = a.shape[0]
    width = p_ref.shape[2]
    col = lax.broadcasted_iota(jnp.int32, (bm, COL_CHUNK), 0) % GRID_W
    for j in range(width // COL_CHUNK):
        c0 = j * COL_CHUNK
        gate_b = jnp.dot(a, w_ref[:, c0:c0 + COL_CHUNK], preferred_element_type=F32)
        gate_c = jnp.dot(a, w_ref[:, width + c0:width + c0 + COL_CHUNK], preferred_element_type=F32)
        hv = jnp.dot(a, w_ref[:, 2 * width + c0:2 * width + c0 + COL_CHUNK], preferred_element_type=F32)
        xg = gate_c * hv
        left = jnp.where(col == 0, 0.0, pltpu.roll(xg, 1, axis=0))
        right = jnp.where(col == GRID_W - 1, 0.0, pltpu.roll(xg, bm - 1, axis=0))
        y = cb_ref[:, c0:c0 + COL_CHUNK] + left * cw_ref[0:1, c0:c0 + COL_CHUNK]
        y = y + xg * cw_ref[1:2, c0:c0 + COL_CHUNK]
        y = y + right * cw_ref[2:3, c0:c0 + COL_CHUNK]
        p_ref[0, :, c0:c0 + COL_CHUNK] = (gate_b * y).astype(BF16)


def _cv_in_proj(x_mid, y_rows, modvec_prev, modvec, g_mix, w_in, conv_w, conv_b, n_lat_blocks):
    b, rows, d = x_mid.shape
    nblk = rows // ROW_BLOCK
    width = conv_w.shape[1]
    t = n_lat_blocks * ROW_BLOCK
    lat_mod = pl.BlockSpec((1, 1, 6, d), lambda bi, i: (bi, 0, 0, 0))
    return pl.pallas_call(
        _cv_in_kernel,
        grid=(b, n_lat_blocks),
        in_specs=[pl.BlockSpec((1, ROW_BLOCK, d), lambda bi, i: (bi, i, 0)),
                  pl.BlockSpec((ROW_BLOCK, d), lambda bi, i: (bi * nblk + i, 0)),
                  lat_mod, lat_mod,
                  _resident((1, d)),
                  _resident(w_in.shape),
                  _resident(conv_w.shape),
                  _resident((1, width))],
        out_specs=pl.BlockSpec((1, ROW_BLOCK, width), lambda bi, i: (bi, i, 0)),
        out_shape=jax.ShapeDtypeStruct((b, t, width), BF16),
        compiler_params=_cparams(2),
        name="cv_in_proj",
    )(x_mid, y_rows, modvec_prev, modvec, g_mix, w_in, conv_w, conv_b)


def _cv_out_kernel(p_ref, x_ref, y_ref, mod0_ref, mod_ref, wout_ref, gffn_ref, wr_ref, br_ref,
                   xo_ref, f_ref, lg_ref):
    x1 = x_ref[0] + mod0_ref[0, 0][5:6] * y_ref[...]
    m = jnp.dot(p_ref[0], wout_ref[...], preferred_element_type=F32)
    _residual_ffn_in(x1, m, mod_ref[0, 0], gffn_ref[...], wr_ref, br_ref, xo_ref, f_ref, lg_ref)


def _cv_out_proj(p, x_mid, y_rows, modvec_prev, modvec, w_out, g_ffn, w_r, b_r):
    b, t, width = p.shape
    d = w_out.shape[1]
    nblk = x_mid.shape[1] // ROW_BLOCK
    row_out = lambda w: (pl.BlockSpec((1, ROW_BLOCK, w), lambda bi, i: (bi, i, 0)),
                         jax.ShapeDtypeStruct((b, t, w), F32))
    outs = [row_out(d), row_out(d), row_out(LANES)]
    lat_mod = pl.BlockSpec((1, 1, 6, d), lambda bi, i: (bi, 0, 0, 0))
    return pl.pallas_call(
        _cv_out_kernel,
        grid=(b, t // ROW_BLOCK),
        in_specs=[pl.BlockSpec((1, ROW_BLOCK, width), lambda bi, i: (bi, i, 0)),
                  pl.BlockSpec((1, ROW_BLOCK, d), lambda bi, i: (bi, i, 0)),
                  pl.BlockSpec((ROW_BLOCK, d), lambda bi, i: (bi * nblk + i, 0)),
                  lat_mod, lat_mod,
                  _resident(w_out.shape),
                  _resident((1, d)),
                  _resident(w_r.shape),
                  _resident((1, LANES))],
        out_specs=[o[0] for o in outs],
        out_shape=[o[1] for o in outs],
        compiler_params=_cparams(2),
        name="cv_out_proj",
    )(p, x_mid, y_rows, modvec_prev, modvec, w_out, g_ffn, w_r, b_r)


def _route_kernel(lg_ref, info_ref, cnt_ref, cnt_scr):
    @pl.when(pl.program_id(0) == 0)
    def _():
        cnt_scr[...] = jnp.zeros_like(cnt_scr)

    lg = lg_ref[...]
    bm = lg.shape[0]
    lane = lax.broadcasted_iota(jnp.int32, lg.shape, 1).astype(F32)
    neg = -jnp.inf

    def first_max(vals):
        mx = jnp.max(vals, axis=1, keepdims=True)
        idx = jnp.min(jnp.where(vals == mx, lane, float(LANES)), axis=1, keepdims=True)
        return mx, idx

    is_grp = lane < N_GROUPS
    m1, grp = first_max(jnp.where(is_grp, lg, neg))
    den = jnp.sum(jnp.where(is_grp, jnp.exp(lg - m1), 0.0), axis=1, keepdims=True)
    p_grp = 1.0 / den
    base = N_GROUPS + EXPERTS_PER_GROUP * grp
    in_grp = (lane >= base) & (lane < base + EXPERTS_PER_GROUP)
    l2 = jnp.where(in_grp, lg, neg)
    v1, i1 = first_max(l2)
    v2, i2 = first_max(jnp.where(lane == i1, neg, l2))
    e = jnp.exp(v2 - v1)
    w_top1 = p_grp * (1.0 / (1.0 + e))
    w_top2 = p_grp * (e / (1.0 + e))
    first_lo = i1 < i2
    lo = jnp.where(first_lo, i1, i2) - base
    hi = jnp.where(first_lo, i2, i1) - base
    w_lo = jnp.where(first_lo, w_top1, w_top2)
    w_hi = jnp.where(first_lo, w_top2, w_top1)
    pair = jnp.where(lo == 0.0, jnp.where(hi == 1.0, 0.0, hi),
                     jnp.where(lo == 1.0, jnp.where(hi == 2.0, 1.0, 4.0), 5.0))
    w_a = jnp.where(pair == 0.0, w_lo, w_hi)
    w_b = jnp.where(pair == 0.0, w_hi, w_lo)
    cls = grp * N_PAIRS + pair

    onehot = (lane == cls).astype(F32)
    tri = (lax.broadcasted_iota(jnp.int32, (bm, bm), 0) > lax.broadcasted_iota(jnp.int32, (bm, bm), 1))
    before = jnp.dot(tri.astype(BF16), onehot.astype(BF16), preferred_element_type=F32)
    rank = jnp.sum(onehot * (before + cnt_scr[...]), axis=1, keepdims=True)
    cnt_scr[...] = cnt_scr[...] + jnp.sum(onehot, axis=0, keepdims=True)
    cnt_ref[...] = cnt_scr[...]
    info = jnp.where(lane == 0, cls,
                     jnp.where(lane == 1, rank,
                               jnp.where(lane == 2, w_a, jnp.where(lane == 3, w_b, 0.0))))
    info_ref[...] = info


def _route(logits):
    n = logits.shape[0]
    return pl.pallas_call(
        _route_kernel,
        grid=(n // ROW_BLOCK,),
        in_specs=[pl.BlockSpec((ROW_BLOCK, LANES), lambda i: (i, 0))],
        out_specs=[pl.BlockSpec((ROW_BLOCK, LANES), lambda i: (i, 0)),
                   pl.BlockSpec((1, LANES), lambda i: (0, 0))],
        out_shape=[jax.ShapeDtypeStruct((n, LANES), F32), jax.ShapeDtypeStruct((1, LANES), F32)],
        scratch_shapes=[pltpu.VMEM((1, LANES), F32)],
        compiler_params=_cparams(1),
        name="moe_route",
    )(logits)


def _dispatch_plan(info, counts, n_tok, expert_base):
    tm = MOE_TILE
    n_tiles = n_tok // tm + N_CLASSES
    n_rows = n_tiles * tm
    i32 = jnp.int32
    slot_a = jnp.asarray(PAIR_SLOT_A, i32)
    slot_b = jnp.asarray(PAIR_SLOT_B, i32)
    cls = info[:, 0].astype(i32)
    rank = info[:, 1].astype(i32)
    cnt = counts[0, :N_CLASSES].astype(i32)
    tiles_c = (cnt + tm - 1) // tm
    tile_end = jnp.cumsum(tiles_c)
    tile_start = tile_end - tiles_c
    pos = tile_start[cls] * tm + rank
    tok = jnp.arange(n_tok, dtype=F32)[:, None]
    per_row = jnp.zeros((n_rows, 3), F32).at[pos].set(jnp.concatenate([tok, info[:, 2:4]], axis=1))
    src = per_row[:, 0].astype(i32).reshape(n_tiles, tm)
    w_rows = per_row[:, 1:3]
    n_used = tile_end[-1]

    def class_of(step, ends):
        return jnp.minimum(jnp.sum((ends[None, :] <= step[:, None]).astype(i32), axis=1), N_CLASSES - 1)

    tile = jnp.arange(n_tiles, dtype=i32)
    tcls = class_of(jnp.minimum(tile, n_used - 1), tile_end)
    e_a = expert_base + (tcls // N_PAIRS) * EXPERTS_PER_GROUP + slot_a[tcls % N_PAIRS]
    e_b = expert_base + (tcls // N_PAIRS) * EXPERTS_PER_GROUP + slot_b[tcls % N_PAIRS]
    n_valid = jnp.clip(cnt[tcls] - (tile - tile_start[tcls]) * tm, 0, tm)
    n_valid = jnp.where(tile < n_used, n_valid, 0).astype(i32)
    row = jnp.arange(tm, dtype=i32)[None, :]
    dump = n_tok + (tile % 2)[:, None] * tm + row
    dst = jnp.where(row < n_valid[:, None], src, dump)

    return dict(src=src.reshape(n_tiles, 1, tm), dst=dst.reshape(n_tiles, 1, tm), w_rows=w_rows,
                n_valid=n_valid, n_tiles=n_tiles, a_runs=_weight_runs(e_a, n_used),
                b_runs=_weight_runs(e_b, n_used), e_a=e_a.astype(i32), e_b=e_b.astype(i32))


def _weight_runs(expert, n_real):
    i32 = jnp.int32
    n = expert.shape[0]
    idx = jnp.arange(n, dtype=i32)
    prev = jnp.concatenate([jnp.full((1,), -1, i32), expert[:-1].astype(i32)])
    first = ((expert != prev) & (idx < n_real)).astype(i32)
    parity = (jnp.cumsum(first) - 1) % 2
    later = (idx[None, :] > idx[:, None]) & (first[None, :] == 1)
    nxt_idx = jnp.min(jnp.where(later, idx[None, :], n), axis=1)
    nxt = jnp.where(nxt_idx < n, expert[jnp.minimum(nxt_idx, n - 1)], -1)
    return first, parity.astype(i32), nxt.astype(i32)


def _row_copy(src_hbm, src_row, dst_buf, dst_row, sem):
    return pltpu.make_async_copy(src_hbm.at[pl.ds(src_row, 1), :], dst_buf.at[pl.ds(dst_row, 1), :], sem)


def _cast_rows(src, dst):
    def body(c, carry):
        rows = pl.ds(pl.multiple_of(c * CAST_ROWS, CAST_ROWS), CAST_ROWS)
        dst[rows, :] = src[rows, :].astype(BF16)
        return carry
    lax.fori_loop(0, src.shape[0] // CAST_ROWS, body, 0)


def _moe_up_kernel(nv_ref, ea_ref, fa_ref, na_ref, eb_ref, fb_ref, nb_ref,
                   src_ref, nsrc_ref, f_hbm, ws_ref, w1_hbm, w3_hbm, h_ref,
                   xbuf, xsem, wf32, wbf, wsem):
    i = pl.program_id(0)
    tm = xbuf.shape[1]
    xs = i % 2
    half = wbf.shape[3]
    runs = ((ea_ref, fa_ref, na_ref), (eb_ref, fb_ref, nb_ref))
    used = lambda tile: nv_ref[jnp.minimum(tile, pl.num_programs(0) - 1)] > 0

    def fetch(s, expert):
        pltpu.make_async_copy(w1_hbm.at[expert], wf32.at[s, 0], wsem.at[s]).start()
        pltpu.make_async_copy(w3_hbm.at[expert], wf32.at[s, 1], wsem.at[s]).start()

    def gather(idx_ref, s):
        for r in range(tm):
            _row_copy(f_hbm, idx_ref[0, 0, r], xbuf.at[s], r, xsem.at[s]).start()

    @pl.when(i == 0)
    def _():
        for s, (e_ref, _, _) in enumerate(runs):
            fetch(s, e_ref[0])
        gather(src_ref, 0)

    @pl.when(used(i))
    def _():
        for s, (_, f_ref, n_ref) in enumerate(runs):
            @pl.when(f_ref[i] == 1)
            def _():
                for m in range(2):
                    pltpu.make_async_copy(w1_hbm.at[0], wf32.at[s, m], wsem.at[s]).wait()
                for m in range(2):
                    _cast_rows(wf32.at[s, m], wbf.at[s, m])

                @pl.when(n_ref[i] >= 0)
                def _():
                    fetch(s, n_ref[i])

        @pl.when((i + 1 < pl.num_programs(0)) & used(i + 1))
        def _():
            gather(nsrc_ref, 1 - xs)

        pltpu.make_async_copy(f_hbm.at[pl.ds(0, tm), :], xbuf.at[xs], xsem.at[xs]).wait()
        x = xbuf[xs].astype(BF16)
        ws = ws_ref[...]
        for s in range(2):
            up = jnp.dot(x, wbf[s, 0], preferred_element_type=F32)
            gate = jnp.dot(x, wbf[s, 1], preferred_element_type=F32)
            act = (up * _sigmoid(up)) * gate
            h_ref[:, s * half:(s + 1) * half] = (act * ws[:, s:s + 1]).astype(BF16)

    @pl.when(jnp.logical_not(used(i)))
    def _():
        h_ref[...] = jnp.zeros_like(h_ref)


def _moe_up(f_rows, plan, w1, w3):
    n_tiles = plan["n_tiles"]
    tm = MOE_TILE
    d = f_rows.shape[1]
    n_e, _, de = w1.shape
    last = n_tiles - 1
    (fa, _, na), (fb, _, nb) = plan["a_runs"], plan["b_runs"]
    grid_spec = pltpu.PrefetchScalarGridSpec(
        num_scalar_prefetch=7,
        grid=(n_tiles,),
        in_specs=[pl.BlockSpec((1, 1, tm), lambda i, *_: (i, 0, 0), memory_space=pltpu.SMEM),
                  pl.BlockSpec((1, 1, tm), lambda i, *_: (jnp.minimum(i + 1, last), 0, 0),
                               memory_space=pltpu.SMEM),
                  pl.BlockSpec(memory_space=pl.ANY),
                  pl.BlockSpec((tm, 2), lambda i, *_: (i, 0)),
                  pl.BlockSpec(memory_space=pl.ANY),
                  pl.BlockSpec(memory_space=pl.ANY)],
        out_specs=pl.BlockSpec((tm, 2 * de), lambda i, *_: (i, 0)),
        scratch_shapes=[pltpu.VMEM((2, tm, d), F32), pltpu.SemaphoreType.DMA((2,)),
                        pltpu.VMEM((2, 2, d, de), F32), pltpu.VMEM((2, 2, d, de), BF16),
                        pltpu.SemaphoreType.DMA((2,))],
    )
    return pl.pallas_call(
        _moe_up_kernel,
        grid_spec=grid_spec,
        out_shape=jax.ShapeDtypeStruct((n_tiles * tm, 2 * de), BF16),
        compiler_params=pltpu.CompilerParams(dimension_semantics=("arbitrary",),
                                             vmem_limit_bytes=MOE_UP_VMEM_LIMIT),
        name="moe_up",
    )(plan["n_valid"], plan["e_a"], fa, na, plan["e_b"], fb, nb,
      plan["src"], plan["src"], f_rows, plan["w_rows"], w1, w3)


def _moe_down_kernel(nv_ref, ea_ref, fa_ref, pa_ref, na_ref, eb_ref, fb_ref, pb_ref, nb_ref,
                     dst_ref, h_ref, w2_hbm, y_hbm, obuf, sems, wf32, wbf, wsem):
    i = pl.program_id(0)
    n_steps = pl.num_programs(0)
    slot = i % 2
    half = wbf.shape[1]
    tm = obuf.shape[1]
    runs = ((ea_ref, fa_ref, pa_ref, na_ref), (eb_ref, fb_ref, pb_ref, nb_ref))

    def fetch(s, expert, buf):
        pltpu.make_async_copy(w2_hbm.at[expert], wf32.at[s, buf], wsem.at[s, buf]).start()

    def drain(tile, s):
        @pl.when(nv_ref[tile] > 0)
        def _():
            pltpu.make_async_copy(obuf.at[s], y_hbm.at[pl.ds(0, tm), :], sems.at[s]).wait()

    @pl.when(i == 0)
    def _():
        n_tok = y_hbm.shape[0] - 2 * tm
        obuf[...] = jnp.zeros_like(obuf)
        for s in range(2):
            fill = pltpu.make_async_copy(obuf.at[s], y_hbm.at[pl.ds(n_tok + s * tm, tm), :], sems.at[s])
            fill.start()
            fill.wait()
        for s, (e_ref, _, _, _) in enumerate(runs):
            fetch(s, e_ref[0], 0)

    @pl.when(i >= 2)
    def _():
        drain(i - 2, slot)

    @pl.when(nv_ref[i] > 0)
    def _():
        for s, (_, f_ref, p_ref, n_ref) in enumerate(runs):
            @pl.when(f_ref[i] == 1)
            def _():
                buf = p_ref[i]
                pltpu.make_async_copy(w2_hbm.at[0], wf32.at[s, buf], wsem.at[s, buf]).wait()

                @pl.when(n_ref[i] >= 0)
                def _():
                    fetch(s, n_ref[i], 1 - buf)
                _cast_rows(wf32.at[s, buf], wbf.at[s])

        out = jnp.dot(h_ref[:, :half], wbf[0], preferred_element_type=F32)
        out = out + jnp.dot(h_ref[:, half:], wbf[1], preferred_element_type=F32)
        obuf[slot] = out
        for r in range(tm):
            _row_copy(obuf.at[slot], r, y_hbm, dst_ref[0, 0, r], sems.at[slot]).start(priority=r % 2)

    @pl.when(i == n_steps - 1)
    def _():
        @pl.when(i >= 1)
        def _():
            drain(i - 1, 1 - slot)
        drain(i, slot)


def _moe_down(h_rows, plan, w2, n_tok):
    n_tiles = plan["n_tiles"]
    tm = MOE_TILE
    n_e, de, d = w2.shape
    grid_spec = pltpu.PrefetchScalarGridSpec(
        num_scalar_prefetch=9,
        grid=(n_tiles,),
        in_specs=[pl.BlockSpec((1, 1, tm), lambda i, *_: (i, 0, 0), memory_space=pltpu.SMEM),
                  pl.BlockSpec((tm, 2 * de), lambda i, *_: (i, 0)),
                  pl.BlockSpec(memory_space=pl.ANY)],
        out_specs=pl.BlockSpec(memory_space=pl.ANY),
        scratch_shapes=[pltpu.VMEM((2, tm, d), F32), pltpu.SemaphoreType.DMA((2,)),
                        pltpu.VMEM((2, 2, de, d), F32), pltpu.VMEM((2, de, d), BF16),
                        pltpu.SemaphoreType.DMA((2, 2))],
    )
    return pl.pallas_call(
        _moe_down_kernel,
        grid_spec=grid_spec,
        out_shape=jax.ShapeDtypeStruct((n_tok + 2 * tm, d), F32),
        compiler_params=_cparams(1),
        name="moe_down",
    )(plan["n_valid"], plan["e_a"], *plan["a_runs"], plan["e_b"], *plan["b_runs"], plan["dst"], h_rows, w2)


def _hier_moe(f_rows, logits, w1, w3, w2, layer):
    n_tok = f_rows.shape[0]
    n_layers, n_exp = w1.shape[:2]
    merged = lambda w: w.reshape((n_layers * n_exp,) + w.shape[2:])
    info, counts = _route(logits)
    plan = _dispatch_plan(info, counts, n_tok, layer * n_exp)
    h_rows = _moe_up(f_rows, plan, merged(w1), merged(w3))
    return _moe_down(h_rows, plan, merged(w2), n_tok)


def _resid_kernel(x_ref, y_ref, mod_ref, o_ref):
    o_ref[0] = x_ref[0] + mod_ref[0, 0][5:6] * y_ref[...]


def _final_kernel(x_ref, y_ref, mod_ref, g_ref, o_ref):
    x = x_ref[0] + mod_ref[0, 0][5:6] * y_ref[...]
    ms = jnp.mean(x * x, axis=-1, keepdims=True)
    o_ref[0] = (x * lax.rsqrt(ms + EPS)) * g_ref[...]


def _gated_residual(x, y_rows, modvec, first_block, n_blocks, mod_row, g_final=None):
    b, rows, d = x.shape
    nblk = rows // ROW_BLOCK
    in_specs = [pl.BlockSpec((1, ROW_BLOCK, d), lambda bi, i: (bi, first_block + i, 0)),
                pl.BlockSpec((ROW_BLOCK, d), lambda bi, i: (bi * nblk + first_block + i, 0)),
                pl.BlockSpec((1, 1, 6, d), lambda bi, i: (bi, mod_row, 0, 0))]
    args = [x, y_rows, modvec]
    kern = _resid_kernel
    if g_final is not None:
        in_specs.append(_resident((1, d)))
        args.append(g_final)
        kern = _final_kernel
    return pl.pallas_call(
        kern,
        grid=(b, n_blocks),
        in_specs=in_specs,
        out_specs=pl.BlockSpec((1, ROW_BLOCK, d), lambda bi, i: (bi, i, 0)),
        out_shape=jax.ShapeDtypeStruct((b, n_blocks * ROW_BLOCK, d), F32),
        compiler_params=_cparams(2),
        name="gated_residual" if g_final is None else "final_norm",
    )(*args)


def _rope_tables(t, ctx_len):
    pos = jnp.arange(t)
    row = (pos // GRID_W).astype(F32)
    col = (pos % GRID_W).astype(F32)
    n_freq = HEAD_DIM // 4
    inv = ROPE_BASE ** (-jnp.arange(n_freq, dtype=F32) / n_freq)
    ang = jnp.concatenate([row[:, None] * inv, col[:, None] * inv], axis=-1)
    cos, sin = jnp.cos(ang), jnp.sin(ang)
    cosf = jnp.concatenate([cos, cos], axis=-1)
    sinf = jnp.concatenate([-sin, sin], axis=-1)
    cosf = jnp.concatenate([cosf, jnp.ones((ctx_len, HEAD_DIM), F32)], axis=0)
    sinf = jnp.concatenate([sinf, jnp.zeros((ctx_len, HEAD_DIM), F32)], axis=0)
    return cosf, sinf


def _retention_tables(decay_logit):
    lg_f = jax.nn.log_sigmoid(decay_logit[0].astype(F32))
    lg_b = jax.nn.log_sigmoid(decay_logit[1].astype(F32))
    n_heads = lg_f.shape[0]
    i = jnp.arange(CHUNK, dtype=F32)
    diff = i[:, None] - i[None, :]
    fwd = jnp.exp(jnp.where(diff >= 0, diff[None] * lg_f[:, None, None], -jnp.inf))
    bwd = jnp.exp(jnp.where(diff < 0, -diff[None] * lg_b[:, None, None], -jnp.inf))
    rows = lambda v: jnp.broadcast_to(v[:, :, None], (n_heads, CHUNK, HEAD_DIM))
    return dict(
        dmat=fwd + bwd,
        qdf=rows(jnp.exp((i + 1.0)[None, :] * lg_f[:, None])),
        qdb=rows(jnp.exp((CHUNK - i)[None, :] * lg_b[:, None])),
        kdf=rows(jnp.exp((CHUNK - 1.0 - i)[None, :] * lg_f[:, None])),
        kdb=rows(jnp.exp(i[None, :] * lg_b[:, None])),
        cdf=jnp.broadcast_to(jnp.exp(CHUNK * lg_f)[:, None, None], (n_heads, 1, HEAD_DIM)),
        cdb=jnp.broadcast_to(jnp.exp(CHUNK * lg_b)[:, None, None], (n_heads, 1, HEAD_DIM)),
    )


def _router_weights(w_r1, b_r1, w_r2, b_r2):
    d = w_r1.shape[0]
    n_exp = N_GROUPS * EXPERTS_PER_GROUP
    w = jnp.concatenate([w_r1, w_r2.transpose(1, 0, 2).reshape(d, n_exp)], axis=1)
    b = jnp.concatenate([b_r1, b_r2.reshape(n_exp)])
    pad = LANES - w.shape[1]
    return jnp.pad(w, ((0, 0), (0, pad))).astype(BF16), jnp.pad(b, (0, pad)).reshape(1, LANES)


def kernel(x, c, ctx, c_ctx, w_mod, b_mod, g_mix, g_ffn, ab_w_in, ab_w_out, ret_decay_logit, sgu_w_s, sgu_b_s,
           cv_w_in, cv_conv_w, cv_conv_b, cv_w_out, moe_w_r1, moe_b_r1, moe_w_r2, moe_b_r2, moe_w1, moe_w3,
           moe_w2, g_final):
    b, t, d = x.shape
    ctx_len = ctx.shape[1]
    n_lat_blocks = t // ROW_BLOCK
    n_heads = ret_decay_logit.shape[2]

    cvec = jnp.zeros((8, d), F32).at[:b].set(c).at[b].set(c_ctx)
    mod = _mod_vectors(cvec, w_mod, b_mod)
    mod_lat = mod[:, :b].reshape(2, b, 1, 6, d)
    mod_ctx = jnp.broadcast_to(mod[:, b].reshape(2, 1, 1, 6, d), (2, b, 1, 6, d))
    modvec = jnp.concatenate([mod_lat, mod_ctx], axis=2)

    assert ctx_len == ROW_BLOCK and t % ROW_BLOCK == 0
    cosf, sinf = _rope_tables(t, ctx_len)
    tabs = _retention_tables(ret_decay_logit[0])
    z_all = _ab_in_proj(x, ctx, modvec[0], g_mix[0:1], ab_w_in[0].astype(BF16), cosf, sinf, n_lat_blocks)
    r_states = _ret_bwd_states(z_all, tabs["kdb"], tabs["cdb"], n_heads)
    b_s_rows = jnp.broadcast_to(sgu_b_s[0][:, :, None], sgu_b_s[0].shape + (HEAD_DIM,))
    w_r, b_r = _router_weights(moe_w_r1[0], moe_b_r1[0], moe_w_r2[0], moe_b_r2[0])
    x_mid, f0, lg0 = _ab_mix(z_all, r_states, x, ctx, modvec[0], ab_w_out[0].astype(BF16),
                             sgu_w_s[0].astype(BF16), b_s_rows, tabs, g_ffn[0:1], w_r, b_r, n_lat_blocks)
    n0 = b * (t + ctx_len)
    y0 = _hier_moe(f0.reshape(n0, d), lg0.reshape(n0, LANES), moe_w1, moe_w3, moe_w2, 0)
    ctx_out = _gated_residual(x_mid, y0, modvec[0], n_lat_blocks, ctx_len // ROW_BLOCK, 1)
    del ctx_out

    p = _cv_in_proj(x_mid, y0, modvec[0], modvec[1], g_mix[1:2], cv_w_in[0].astype(BF16), cv_conv_w[0],
                    cv_conv_b[0:1], n_lat_blocks)
    w_r, b_r = _router_weights(moe_w_r1[1], moe_b_r1[1], moe_w_r2[1], moe_b_r2[1])
    x2, f1, lg1 = _cv_out_proj(p, x_mid, y0, modvec[0], modvec[1], cv_w_out[0].astype(BF16), g_ffn[1:2],
                               w_r, b_r)
    y1 = _hier_moe(f1.reshape(b * t, d), lg1.reshape(b * t, LANES), moe_w1, moe_w3, moe_w2, 1)
    return _gated_residual(x2, y1, modvec[1], 0, n_lat_blocks, 0, g_final.reshape(1, d))
```
